```python
import functools
import jax
import jax.numpy as jnp
from jax import lax
import numpy as np

D_MODEL = 1024
BATCH = 1
SEQ = 16384
DEPTH = 1
DEC_BATCH = 128
DEC_SEQ = 8
PAST_LEN = 8192
PAGE_SIZE = 128

N_HEADS = 8
N_KV_HEADS = 4
HEAD_DIM = 64
ATTN_W = N_HEADS * HEAD_DIM
KV_W = N_KV_HEADS * HEAD_DIM
ROPE_THETA = 500000.0
IDX_HEADS = 8
IDX_DIM = 64
TOPK_MAX = 256
Q_BLOCK = 128
CONV_W = 512
CONV_K = 3
N_GROUPS = 4
EXP_PER_GROUP = 4
N_EXPERTS = N_GROUPS * EXP_PER_GROUP
EXPERT_FF = 256
TOP_K_EXP = 2
EPS = 1e-6
POOL_NUM = 5
POOL_DEN = 4
PROJ_SPLITS = (ATTN_W, KV_W, KV_W, IDX_HEADS * IDX_DIM, IDX_DIM, IDX_HEADS,
               CONV_W, CONV_W, CONV_W, D_MODEL, D_MODEL)
IN_PROJ_W = ATTN_W + 2 * KV_W + IDX_HEADS * IDX_DIM + IDX_DIM + IDX_HEADS + 3 * CONV_W + 2 * D_MODEL

kernel_name = 'hybrid_dsa_shortconv_hmoe_step'


def _split_cols(p):
    outs, start = [], 0
    for w in PROJ_SPLITS:
        outs.append(p[..., start:start + w])
        start += w
    return outs


def _rmsnorm(x):
    xf = x.astype(jnp.float32)
    return (xf * lax.rsqrt(jnp.mean(xf * xf, axis=-1, keepdims=True) + EPS)).astype(x.dtype)


def _modulated_norm(x, g, shift, scale):
    return _rmsnorm(x) * g * (1 + scale[:, None, :]) + shift[:, None, :]


def _partial_rope(x, pos):
    rot_dim = x.shape[-1] // 4
    half = rot_dim // 2
    inv_freq = ROPE_THETA ** (-jnp.arange(half, dtype=jnp.float32) / half)
    ang = pos.astype(jnp.float32)[:, None] * inv_freq
    cos = jnp.cos(ang)[:, None, :]
    sin = jnp.sin(ang)[:, None, :]
    xr = x[..., :rot_dim].astype(jnp.float32)
    x1, x2 = xr[..., :half], xr[..., half:]
    rot = jnp.concatenate([x1 * cos - x2 * sin, x2 * cos + x1 * sin], axis=-1).astype(x.dtype)
    return jnp.concatenate([rot, x[..., rot_dim:]], axis=-1)


def _mixer_projections(h, pos, w_in):
    B, T, _ = h.shape
    q, k, v, iq, ik, iw, u, gate_b, gate_c, ga, gb = _split_cols(h @ w_in)
    q = _partial_rope(q.reshape(B, T, N_HEADS, HEAD_DIM), pos)
    k = _partial_rope(k.reshape(B, T, N_KV_HEADS, HEAD_DIM), pos)
    v = v.reshape(B, T, N_KV_HEADS, HEAD_DIM)
    iq = _partial_rope(iq.reshape(B, T, IDX_HEADS, IDX_DIM), pos)
    ik = _partial_rope(ik[:, :, None, :], pos)[:, :, 0, :]
    return q, k, v, iq, ik, iw, u, gate_b, gate_c, ga, gb


def _index_scores(iq, iw, ik):
    s = jnp.einsum('bqhd,bld->bqhl', iq, ik).astype(jnp.float32) * (IDX_DIM ** -0.5)
    return jnp.einsum('bqhl,bqh->bql', jax.nn.relu(s), iw.astype(jnp.float32)) * (IDX_HEADS ** -0.5)


def _sparse_attend(q, k_sel, v_sel, valid):
    B, Q = q.shape[:2]
    qg = q.reshape(B, Q, N_KV_HEADS, N_HEADS // N_KV_HEADS, HEAD_DIM)
    s = jnp.einsum('bqgrd,bqkgd->bqgrk', qg, k_sel).astype(jnp.float32) * (HEAD_DIM ** -0.5)
    s = jnp.where(valid[:, :, None, None, :], s, -jnp.inf)
    p = jax.nn.softmax(s, axis=-1).astype(v_sel.dtype)
    o = jnp.einsum('bqgrk,bqkgd->bqgrd', p, v_sel)
    return o.reshape(B, Q, ATTN_W)


def _gather_rows(rows, idx):
    return jax.vmap(lambda rb, ib: rb[ib])(rows, idx)


def _prompt_sparse_attention(q, k, v, iq, iw, ik):
    B, S = q.shape[:2]
    topk = min(TOPK_MAX, S // 4)
    nb = S // Q_BLOCK
    key_pos = jnp.arange(S)

    def block(args):
        qb, iqb, iwb, start = args
        qpos = start + jnp.arange(Q_BLOCK)
        sc = _index_scores(iqb, iwb, ik)
        causal = key_pos[None, :] <= qpos[:, None]
        sc = jnp.where(causal[None], sc, -jnp.inf)
        _, idx = lax.top_k(sc, topk)
        valid = idx <= qpos[None, :, None]
        return _sparse_attend(qb, _gather_rows(k, idx), _gather_rows(v, idx), valid)

    to_blocks = lambda a: jnp.moveaxis(a.reshape(B, nb, Q_BLOCK, *a.shape[2:]), 1, 0)
    out = lax.map(block, (to_blocks(q), to_blocks(iq), to_blocks(iw), jnp.arange(nb) * Q_BLOCK))
    return jnp.moveaxis(out, 0, 1).reshape(B, S, ATTN_W)


def _sample_sparse_attention(q, k, v, iq, iw, ik, cache_k, cache_v, cache_idx_k, page_table):
    DB, T = q.shape[:2]
    n_pages = page_table.shape[1]
    past = n_pages * PAGE_SIZE
    L = past + T
    topk = min(TOPK_MAX, L // 4)
    ik_past = cache_idx_k[page_table].reshape(DB, past, IDX_DIM)
    ik_all = jnp.concatenate([ik_past, ik], axis=1)
    sc = _index_scores(iq, iw, ik_all)
    qpos = past + jnp.arange(T)
    causal = jnp.arange(L)[None, :] <= qpos[:, None]
    sc = jnp.where(causal[None], sc, -jnp.inf)
    _, idx = lax.top_k(sc, topk)
    valid = idx <= qpos[None, :, None]
    is_past = (idx < past)[..., None, None]
    pidx = jnp.minimum(idx, past - 1)
    phys = jax.vmap(lambda pt, pg: pt[pg])(page_table, pidx // PAGE_SIZE)
    off = pidx % PAGE_SIZE
    nidx = jnp.clip(idx - past, 0, T - 1)
    k_sel = jnp.where(is_past, cache_k[phys, off], _gather_rows(k, nidx))
    v_sel = jnp.where(is_past, cache_v[phys, off], _gather_rows(v, nidx))
    return _sparse_attend(q, k_sel, v_sel, valid)


def _short_conv(u, buf, conv_w):
    T = u.shape[1]
    ext = jnp.concatenate([buf, u], axis=1)
    y = ext[:, 0:T] * conv_w[0]
    for j in range(1, CONV_K):
        y = y + ext[:, j:j + T] * conv_w[j]
    return y, ext[:, -(CONV_K - 1):]


def _hier_moe(h, w_rg, b_rg, w_re, b_re, w_eg, w_eu, w_ed):
    B, T, D = h.shape
    xt = h.reshape(B * T, D)
    n = xt.shape[0]
    rows = jnp.arange(n)
    g_logit = (xt @ w_rg + b_rg).astype(jnp.float32)
    g_prob = jax.nn.softmax(g_logit, axis=-1)
    g_sel = jnp.argmax(g_logit, axis=-1)
    g_w = g_prob[rows, g_sel][:, None]
    e_logit = (xt @ w_re + b_re).astype(jnp.float32).reshape(n, N_GROUPS, EXP_PER_GROUP)
    e_in_grp = e_logit[rows, g_sel]
    top_v, top_i = lax.top_k(e_in_grp, TOP_K_EXP)
    top_w = jax.nn.softmax(top_v, axis=-1) * g_w
    e_idx = g_sel[:, None] * EXP_PER_GROUP + top_i
    combine = jnp.einsum('nke,nk->ne', jax.nn.one_hot(e_idx, N_EXPERTS, dtype=jnp.float32), top_w).astype(h.dtype)
    y = jnp.zeros_like(xt)
    for e in range(N_EXPERTS):
        hid = jax.nn.silu(xt @ w_eg[e]) * (xt @ w_eu[e])
        y = y + combine[:, e:e + 1] * (hid @ w_ed[e])
    return y.reshape(B, T, D)


def _layer(x, c, pos, attend, conv_buf, w_ada, b_ada, norm1_g, norm2_g, w_in_mix, conv_w,
           w_o_attn, w_o_conv, w_out, w_rg, b_rg, w_re, b_re, w_eg, w_eu, w_ed):
    mod = jax.nn.silu(c) @ w_ada + b_ada
    sh1, sc1, g1, sh2, sc2, g2 = jnp.split(mod, 6, axis=-1)
    h = _modulated_norm(x, norm1_g, sh1, sc1)
    q, k, v, iq, ik, iw, u, gate_b, gate_c, ga, gb = _mixer_projections(h, pos, w_in_mix)
    y_attn = attend(q, k, v, iq, iw, ik) @ w_o_attn
    conv_y, new_buf = _short_conv(gate_c * u, conv_buf, conv_w)
    y_conv = (gate_b * conv_y) @ w_o_conv
    mixed = (jax.nn.sigmoid(ga) * y_attn + jax.nn.sigmoid(gb) * y_conv) @ w_out
    x = x + g1[:, None, :] * mixed
    h2 = _modulated_norm(x, norm2_g, sh2, sc2)
    x = x + g2[:, None, :] * _hier_moe(h2, w_rg, b_rg, w_re, b_re, w_eg, w_eu, w_ed)
    return x, k, v, ik, new_buf


def setup_inputs(seed: int = 0) -> dict:
    key = jax.random.key(seed)
    ks = jax.random.split(key, 28)
    f32 = jnp.float32
    n_pages = PAST_LEN // PAGE_SIZE
    n_pool = (DEC_BATCH * n_pages * POOL_NUM) // POOL_DEN
    nrm = lambda k, shape, s: s * jax.random.normal(k, shape, f32)
    page_table = jax.random.permutation(ks[6], n_pool)[: DEC_BATCH * n_pages].reshape(DEC_BATCH, n_pages).astype(jnp.int32)
    return {
        'x_prompt': nrm(ks[0], (BATCH, SEQ, D_MODEL), 1.0),
        'x_sample': nrm(ks[1], (DEC_BATCH, DEC_SEQ, D_MODEL), 1.0),
        'cache_k': nrm(ks[2], (DEPTH, n_pool, PAGE_SIZE, N_KV_HEADS, HEAD_DIM), 1.0),
        'cache_v': nrm(ks[3], (DEPTH, n_pool, PAGE_SIZE, N_KV_HEADS, HEAD_DIM), 1.0),
        'cache_idx_k': nrm(ks[4], (DEPTH, n_pool, PAGE_SIZE, IDX_DIM), 1.0),
        'state_conv': nrm(ks[5], (DEPTH, DEC_BATCH, CONV_K - 1, CONV_W), 0.5),
        'page_table': page_table,
        'c_prompt': nrm(ks[7], (BATCH, D_MODEL), 1.0),
        'c_sample': nrm(ks[8], (DEC_BATCH, D_MODEL), 1.0),
        'w_ada': nrm(ks[9], (DEPTH, D_MODEL, 6 * D_MODEL), 0.5 * D_MODEL ** -0.5),
        'b_ada': nrm(ks[10], (DEPTH, 6 * D_MODEL), 0.1),
        'norm1_g': 1.0 + nrm(ks[11], (DEPTH, D_MODEL), 0.05),
        'norm2_g': 1.0 + nrm(ks[12], (DEPTH, D_MODEL), 0.05),
        'w_in_mix': nrm(ks[13], (DEPTH, D_MODEL, IN_PROJ_W), D_MODEL ** -0.5),
        'conv_w': nrm(ks[14], (DEPTH, CONV_K, CONV_W), 0.5),
        'w_o_attn': nrm(ks[15], (DEPTH, ATTN_W, D_MODEL), ATTN_W ** -0.5),
        'w_o_conv': nrm(ks[16], (DEPTH, CONV_W, D_MODEL), CONV_W ** -0.5),
        'w_out': nrm(ks[17], (DEPTH, D_MODEL, D_MODEL), D_MODEL ** -0.5),
        'w_router_group': nrm(ks[18], (DEPTH, D_MODEL, N_GROUPS), D_MODEL ** -0.5),
        'b_router_group': nrm(ks[19], (DEPTH, N_GROUPS), 0.01),
        'w_router_expert': nrm(ks[20], (DEPTH, D_MODEL, N_EXPERTS), D_MODEL ** -0.5),
        'b_router_expert': nrm(ks[21], (DEPTH, N_EXPERTS), 0.01),
        'w_exp_gate': nrm(ks[22], (DEPTH, N_EXPERTS, D_MODEL, EXPERT_FF), D_MODEL ** -0.5),
        'w_exp_up': nrm(ks[23], (DEPTH, N_EXPERTS, D_MODEL, EXPERT_FF), D_MODEL ** -0.5),
        'w_exp_down': nrm(ks[24], (DEPTH, N_EXPERTS, EXPERT_FF, D_MODEL), EXPERT_FF ** -0.5),
        'final_norm_g': 1.0 + nrm(ks[25], (D_MODEL,), 0.05),
    }


def reference(x_prompt, x_sample, cache_k, cache_v, cache_idx_k, state_conv, page_table, c_prompt, c_sample,
              w_ada, b_ada, norm1_g, norm2_g, w_in_mix, conv_w, w_o_attn, w_o_conv, w_out,
              w_router_group, b_router_group, w_router_expert, b_router_expert,
              w_exp_gate, w_exp_up, w_exp_down, final_norm_g):
    B, S, _ = x_prompt.shape
    DB, T, _ = x_sample.shape
    past = page_table.shape[1] * PAGE_SIZE
    pos_p = jnp.arange(S)
    pos_s = past + jnp.arange(T)
    xp, xs = x_prompt, x_sample
    kp, vp, ikp, cvp, ksm, vsm, iks, cvs = [], [], [], [], [], [], [], []
    for l in range(DEPTH):
        lw = (w_ada[l], b_ada[l], norm1_g[l], norm2_g[l], w_in_mix[l], conv_w[l], w_o_attn[l], w_o_conv[l],
              w_out[l], w_router_group[l], b_router_group[l], w_router_expert[l], b_router_expert[l],
              w_exp_gate[l], w_exp_up[l], w_exp_down[l])
        zero_buf = jnp.zeros((B, CONV_K - 1, CONV_W), x_prompt.dtype)
        xp, k1, v1, ik1, buf1 = _layer(xp, c_prompt, pos_p, _prompt_sparse_attention, zero_buf, *lw)
        attend_s = functools.partial(_sample_sparse_attention, cache_k=cache_k[l], cache_v=cache_v[l],
                                     cache_idx_k=cache_idx_k[l], page_table=page_table)
        xs, k2, v2, ik2, buf2 = _layer(xs, c_sample, pos_s, attend_s, state_conv[l], *lw)
        kp.append(k1); vp.append(v1); ikp.append(ik1); cvp.append(buf1)
        ksm.append(k2); vsm.append(v2); iks.append(ik2); cvs.append(buf2)
    y_prompt = _rmsnorm(xp) * final_norm_g
    y_sample = _rmsnorm(xs) * final_norm_g
    return (y_prompt, y_sample, jnp.stack(kp), jnp.stack(vp), jnp.stack(ikp), jnp.stack(cvp),
            jnp.stack(ksm), jnp.stack(vsm), jnp.stack(iks), jnp.stack(cvs))
```

```python
import functools

import jax
import jax.numpy as jnp
import numpy as np
from jax import lax
from jax.experimental import pallas as pl
from jax.experimental.pallas import tpu as pltpu

F32 = jnp.float32
BF16 = jnp.bfloat16
I32 = jnp.int32

D_MODEL = 1024
N_HEADS = 8
N_KV_HEADS = 4
HEAD_DIM = 64
ATTN_W = N_HEADS * HEAD_DIM
KV_W = N_KV_HEADS * HEAD_DIM
ROPE_THETA = 500000.0
IDX_HEADS = 8
IDX_DIM = 64
TOPK_MAX = 256
CONV_W = 512
CONV_K = 3
N_GROUPS = 4
EXP_PER_GROUP = 4
N_EXPERTS = N_GROUPS * EXP_PER_GROUP
EXPERT_FF = 256
EPS = 1e-6
PAGE_SIZE = 128

LANES = 128
SUBLANES = 8
VMEM_LIMIT = 56 * 1024 * 1024
INT_MIN = -(2 ** 31)
INT_BIG = 2 ** 30
IW_LANE = 112
ROUTER_E0 = 4


def _dot(a, b):
    return jnp.dot(a, b, preferred_element_type=F32)


def _dot_t(a, b):
    return lax.dot_general(a, b, (((1,), (1,)), ((), ())), preferred_element_type=F32)


def _split_bf16(a):
    hi = a.astype(BF16)
    lo = (a - hi.astype(F32)).astype(BF16)
    return hi, lo


def _dot3(a, w_hi, w_lo):
    a_hi, a_lo = _split_bf16(a)
    return _dot(a_hi, w_hi) + (_dot(a_hi, w_lo) + _dot(a_lo, w_hi))


def _sigmoid(x):
    return 1.0 / (1.0 + jnp.exp(-x))


def _params(*sem):
    return pltpu.CompilerParams(dimension_semantics=sem, vmem_limit_bytes=VMEM_LIMIT)


def _resident(shape):
    nd = len(shape)
    return pl.BlockSpec(shape, lambda *_: (0,) * nd, pipeline_mode=pl.Buffered(1))


def _ada_kernel(c_ref, w_ref, b_ref, o_ref):
    c = c_ref[...]
    a = c * _sigmoid(c)
    w = w_ref[...]
    w_hi, w_lo = _split_bf16(w)
    o_ref[...] = _dot3(a, w_hi, w_lo) + b_ref[...]


def _ada(c_all, w_ada, b_ada):
    rows = c_all.shape[0]
    n = w_ada.shape[1]
    tn = 512
    return pl.pallas_call(
        _ada_kernel,
        grid=(n // tn,),
        in_specs=[pl.BlockSpec((rows, D_MODEL), lambda j: (0, 0)),
                  pl.BlockSpec((D_MODEL, tn), lambda j: (0, j)),
                  pl.BlockSpec((1, tn), lambda j: (0, j))],
        out_specs=pl.BlockSpec((rows, tn), lambda j: (0, j)),
        out_shape=jax.ShapeDtypeStruct((rows, n), F32),
        compiler_params=_params("arbitrary"),
        name="ada",
    )(c_all, w_ada, b_ada.reshape(1, n))


def _modnorm(x3, g, sc3, sh3):
    ms = jnp.mean(x3 * x3, axis=-1, keepdims=True)
    xn = x3 * lax.rsqrt(ms + EPS)
    h = xn * g[None] * (1.0 + sc3) + sh3
    return h.reshape(x3.shape[0] * x3.shape[1], x3.shape[2])


def _proj_kernel(*refs, sample, tm, pos_base, cols):
    if sample:
        (x_ref, sh_ref, sc_ref, g_ref, w_ref, invf_ref, cw_ref, woc_ref, st_ref,
         qo_ref, iqo_ref, iks_ref, k_ref, v_ref, kb_ref, vb_ref, yc_ref, sga_ref, cst_ref) = refs
    else:
        (x_ref, sh_ref, sc_ref, g_ref, w_ref, invf_ref, cw_ref, woc_ref,
         qo_ref, iqo_ref, iks_ref, k_ref, v_ref, kt_ref, ikt_ref, vb_ref, yc_ref, sga_ref, cst_ref,
         cu_scr) = refs
    i = pl.program_id(0)
    hb = _modnorm(x_ref[...], g_ref[...], sc_ref[...], sh_ref[...]).astype(BF16)

    def proj(name):
        c0, c1 = cols[name]
        return _dot(hb, w_ref[:, c0:c1])

    lane = lax.broadcasted_iota(I32, (tm, LANES), 1)
    row = lax.broadcasted_iota(I32, (tm, LANES), 0)
    if sample:
        pos = pos_base + (row & (SUBLANES - 1))
    else:
        pos = i * tm + row
    ang = pos.astype(F32) * invf_ref[...]
    cosv = jnp.cos(ang)
    sinv = jnp.sin(ang)
    j = lane & (HEAD_DIM - 1)
    half = HEAD_DIM // 8
    rc = jnp.where(j < 2 * half, cosv, 1.0)
    rs_lo = jnp.where((j >= half) & (j < 2 * half), sinv, 0.0)
    rs_hi = jnp.where(j < half, -sinv, 0.0)

    def rope(xs):
        return xs * rc + pltpu.roll(xs, half, 1) * rs_lo + pltpu.roll(xs, LANES - half, 1) * rs_hi

    pq = proj("q")
    if sample:
        for h in range(N_HEADS):
            blk = pq[:, KV_W * h:KV_W * (h + 1)]
            r = jnp.concatenate([rope(blk[:, :LANES]), rope(blk[:, LANES:])], axis=1) * (HEAD_DIM ** -0.5)
            qo_ref[:, h] = r.reshape(tm // SUBLANES, SUBLANES, KV_W).astype(BF16)
    else:
        for s in range(ATTN_W // LANES):
            r = (rope(pq[:, LANES * s:LANES * (s + 1)]) * (HEAD_DIM ** -0.5)).astype(BF16)
            qo_ref[2 * s] = r[:, :HEAD_DIM]
            qo_ref[2 * s + 1] = r[:, HEAD_DIM:]

    pk = proj("k")
    for s in range(KV_W // LANES):
        rk = rope(pk[:, LANES * s:LANES * (s + 1)])
        k_ref[:, LANES * s:LANES * (s + 1)] = rk
        if sample:
            kb_ref[:, LANES * s:LANES * (s + 1)] = rk.astype(BF16)
        else:
            kt_ref[LANES * s:LANES * (s + 1), :] = rk.T.astype(BF16)

    pv = proj("v")
    v_ref[...] = pv
    vb_ref[...] = pv.astype(BF16)

    piq = proj("iq")
    for s in range(IDX_HEADS * IDX_DIM // LANES):
        r = (rope(piq[:, LANES * s:LANES * (s + 1)]) * (IDX_DIM ** -0.5)).astype(BF16)
        if sample:
            iqo_ref[:, 2 * s] = r[:, :IDX_DIM].reshape(tm // SUBLANES, SUBLANES, IDX_DIM)
            iqo_ref[:, 2 * s + 1] = r[:, IDX_DIM:].reshape(tm // SUBLANES, SUBLANES, IDX_DIM)
        else:
            iqo_ref[2 * s] = r[:, :IDX_DIM]
            iqo_ref[2 * s + 1] = r[:, IDX_DIM:]

    slab = rope(proj("iks"))
    iks_ref[...] = slab
    if not sample:
        ikt_ref[...] = slab.T[:IDX_DIM, :].astype(BF16)

    cu = proj("gc") * proj("u")
    cw = cw_ref[...]
    if sample:
        g8 = tm // SUBLANES
        st = st_ref[...]
        grp = (g8, SUBLANES, CONV_W)
        s0 = jnp.broadcast_to(st[:, 0:1, :], grp).reshape(tm, CONV_W)
        s1 = jnp.broadcast_to(st[:, 1:2, :], grp).reshape(tm, CONV_W)
        t = lax.broadcasted_iota(I32, (tm, CONV_W), 0) & (SUBLANES - 1)
        m1 = jnp.where(t >= 1, pltpu.roll(cu, 1, 0), s1)
        m2 = jnp.where(t >= 2, pltpu.roll(cu, 2, 0), jnp.where(t == 1, s1, s0))
        conv = m2 * cw[0:1] + m1 * cw[1:2] + cu * cw[2:3]
        cst_ref[...] = cu.reshape(grp)
    else:
        @pl.when(i == 0)
        def _():
            cu_scr[0:SUBLANES, :] = jnp.zeros((SUBLANES, CONV_W), F32)
        cu_scr[SUBLANES:SUBLANES + tm, :] = cu
        m1 = cu_scr[SUBLANES - 1:SUBLANES - 1 + tm, :]
        m2 = cu_scr[SUBLANES - 2:SUBLANES - 2 + tm, :]
        conv = m2 * cw[0:1] + m1 * cw[1:2] + cu * cw[2:3]
        tail = cu_scr[tm:tm + SUBLANES, :]
        cu_scr[0:SUBLANES, :] = tail
        cst_ref[...] = tail
    zc = (proj("gb_conv") * conv).astype(BF16)
    yconv = _dot(zc, woc_ref[...])
    sga_ref[...] = _sigmoid(proj("ga"))
    yc_ref[...] = _sigmoid(proj("gb")) * yconv


def _proj_layout(sample):
    names = [("q", N_HEADS * KV_W if sample else ATTN_W), ("k", KV_W), ("v", KV_W),
             ("iq", IDX_HEADS * IDX_DIM), ("iks", LANES), ("u", CONV_W), ("gb_conv", CONV_W), ("gc", CONV_W),
             ("ga", D_MODEL), ("gb", D_MODEL)]
    cols, c = {}, 0
    for n, w in names:
        cols[n] = (c, c + w)
        c += w
    return cols, c


def _proj_weight(w_in, sample):
    splits = (ATTN_W, KV_W, KV_W, IDX_HEADS * IDX_DIM, IDX_DIM, IDX_HEADS, CONV_W, CONV_W, CONV_W, D_MODEL, D_MODEL)
    parts, c = [], 0
    for w in splits:
        parts.append(w_in[:, c:c + w])
        c += w
    wq, wk, wv, wiq, wik, wiw, wu, wgb, wgc, wga, wgbr = parts
    z = lambda n: jnp.zeros((D_MODEL, n), w_in.dtype)
    if sample:
        qparts = []
        for h in range(N_HEADS):
            g = h // (N_HEADS // N_KV_HEADS)
            qparts += [z(HEAD_DIM * g), wq[:, HEAD_DIM * h:HEAD_DIM * (h + 1)], z(KV_W - HEAD_DIM * (g + 1))]
        wq = jnp.concatenate(qparts, axis=1)
    slab = jnp.concatenate([wik, z(IW_LANE - IDX_DIM), wiw, z(LANES - IW_LANE - IDX_HEADS)], axis=1)
    return jnp.concatenate([wq, wk, wv, wiq, slab, wu, wgb, wgc, wga, wgbr], axis=1).astype(BF16)


def _proj(x3, mod3, mod_row0, norm_g, w_p, invf, conv_w, w_oc, state, *, sample, pos_base):
    G, R, _ = x3.shape
    n = G * R
    cols, wtot = _proj_layout(sample)
    if sample:
        bb = min(G, 64)
        tm = bb * R
        nb = G // bb
        xspec = pl.BlockSpec((bb, R, D_MODEL), lambda i: (i, 0, 0))
        shspec = pl.BlockSpec((bb, 1, D_MODEL), lambda i: (i, 0, 0))
        scspec = pl.BlockSpec((bb, 1, D_MODEL), lambda i: (i, 0, 1))
    else:
        tm = min(R, 512)
        nb = R // tm
        xspec = pl.BlockSpec((1, tm, D_MODEL), lambda i: (0, i, 0))
        shspec = pl.BlockSpec((1, 1, D_MODEL), lambda i: (mod_row0, 0, 0))
        scspec = pl.BlockSpec((1, 1, D_MODEL), lambda i: (mod_row0, 0, 1))
    row = lambda w: pl.BlockSpec((tm, w), lambda i: (i, 0))
    in_specs = [xspec, shspec, scspec, _resident((1, D_MODEL)), _resident((D_MODEL, wtot)), _resident((1, LANES)),
                _resident((CONV_K, CONV_W)), _resident((CONV_W, D_MODEL))]
    args = [x3, mod3, mod3, norm_g.reshape(1, D_MODEL), w_p, invf, conv_w, w_oc]
    if sample:
        in_specs.append(pl.BlockSpec((bb, CONV_K - 1, CONV_W), lambda i: (i, 0, 0)))
        args.append(state)
        out_specs = [pl.BlockSpec((bb, N_HEADS, R, KV_W), lambda i: (i, 0, 0, 0)),
                     pl.BlockSpec((bb, IDX_HEADS, R, IDX_DIM), lambda i: (i, 0, 0, 0)),
                     row(LANES), row(KV_W), row(KV_W), row(KV_W), row(KV_W), row(D_MODEL), row(D_MODEL),
                     pl.BlockSpec((bb, R, CONV_W), lambda i: (i, 0, 0))]
        out_shape = [jax.ShapeDtypeStruct((G, N_HEADS, R, KV_W), BF16),
                     jax.ShapeDtypeStruct((G, IDX_HEADS, R, IDX_DIM), BF16),
                     jax.ShapeDtypeStruct((n, LANES), F32),
                     jax.ShapeDtypeStruct((n, KV_W), F32), jax.ShapeDtypeStruct((n, KV_W), F32),
                     jax.ShapeDtypeStruct((n, KV_W), BF16), jax.ShapeDtypeStruct((n, KV_W), BF16),
                     jax.ShapeDtypeStruct((n, D_MODEL), F32), jax.ShapeDtypeStruct((n, D_MODEL), F32),
                     jax.ShapeDtypeStruct((G, R, CONV_W), F32)]
        scratch = []
    else:
        out_specs = [pl.BlockSpec((N_HEADS, tm, HEAD_DIM), lambda i: (0, i, 0)),
                     pl.BlockSpec((IDX_HEADS, tm, IDX_DIM), lambda i: (0, i, 0)),
                     row(LANES), row(KV_W), row(KV_W),
                     pl.BlockSpec((KV_W, tm), lambda i: (0, i)), pl.BlockSpec((IDX_DIM, tm), lambda i: (0, i)),
                     row(KV_W), row(D_MODEL), row(D_MODEL),
                     pl.BlockSpec((SUBLANES, CONV_W), lambda i: (0, 0))]
        out_shape = [jax.ShapeDtypeStruct((N_HEADS, n, HEAD_DIM), BF16),
                     jax.ShapeDtypeStruct((IDX_HEADS, n, IDX_DIM), BF16),
                     jax.ShapeDtypeStruct((n, LANES), F32),
                     jax.ShapeDtypeStruct((n, KV_W), F32), jax.ShapeDtypeStruct((n, KV_W), F32),
                     jax.ShapeDtypeStruct((KV_W, n), BF16), jax.ShapeDtypeStruct((IDX_DIM, n), BF16),
                     jax.ShapeDtypeStruct((n, KV_W), BF16),
                     jax.ShapeDtypeStruct((n, D_MODEL), F32), jax.ShapeDtypeStruct((n, D_MODEL), F32),
                     jax.ShapeDtypeStruct((SUBLANES, CONV_W), F32)]
        scratch = [pltpu.VMEM((tm + SUBLANES, CONV_W), F32)]
    return pl.pallas_call(
        functools.partial(_proj_kernel, sample=sample, tm=tm, pos_base=pos_base, cols=cols),
        grid=(nb,), in_specs=in_specs, out_specs=out_specs, out_shape=out_shape, scratch_shapes=scratch,
        compiler_params=_params("arbitrary"),
        name="proj_sample" if sample else "proj_prompt",
    )(*args)


def _sort_key(score):
    score = jnp.where(score == 0.0, 0.0, score)
    bits = pltpu.bitcast(score, I32)
    return bits ^ ((bits >> 31) & 0x7FFFFFFF)


def _lane_bcast(col):
    return jnp.broadcast_to(col, (col.shape[0], LANES))


def _kth_largest(count_ge, rows, topk):
    def bit_step(b, t):
        cand = t + lax.shift_left(jnp.int32(1), 31 - b)
        cnt = count_ge(_lane_bcast(cand))
        return jnp.where(cnt >= topk, cand, t)
    return lax.fori_loop(0, 32, bit_step, jnp.full((rows, 1), INT_MIN, I32))


def _tie_cut(count_tie_below, need, r, rows, nbits):
    def bit_step(b, x):
        cand = x + lax.shift_left(jnp.int32(1), nbits - 1 - b)
        cnt = count_tie_below(_lane_bcast(cand))
        return jnp.where(cnt < r, cand, x)
    x = lax.fori_loop(0, nbits, bit_step, jnp.zeros((rows, 1), I32))
    return jnp.where(need, x, INT_BIG)


def _attn_prompt_kernel(iq_ref, q_ref, iks_ref, ikt_ref, kt_ref, vb_ref, o_ref,
                        keys_scr, wb_scr, cut_scr, m_scr, l_scr, acc_scr, *, tq, ck, topk, nbits):
    qi = pl.program_id(0)
    n_ck = (qi * tq + tq + ck - 1) // ck
    nsub = ck // LANES
    rep = N_HEADS // N_KV_HEADS

    w = iks_ref[:, IW_LANE:IW_LANE + IDX_HEADS] * (IDX_HEADS ** -0.5)
    for h in range(IDX_HEADS):
        wb_scr[h] = _lane_bcast(w[:, h:h + 1])
    qpos = qi * tq + lax.broadcasted_iota(I32, (tq, LANES), 0)
    lane = lax.broadcasted_iota(I32, (tq, LANES), 1)

    def score_chunk(c, carry):
        k0 = pl.multiple_of(c * ck, ck)
        ikc = ikt_ref[:, pl.ds(k0, ck)]
        acc = jnp.zeros((tq, ck), F32)
        for h in range(IDX_HEADS):
            s = _dot(iq_ref[h], ikc)
            wbh = wb_scr[h]
            acc = acc + jnp.maximum(s, 0.0) * jnp.concatenate([wbh] * nsub, axis=1)
        key = _sort_key(acc)
        for jj in range(nsub):
            kidx = k0 + jj * LANES + lane
            keys_scr[:, pl.ds(k0 + jj * LANES, LANES)] = jnp.where(
                kidx <= qpos, key[:, jj * LANES:(jj + 1) * LANES], INT_MIN)
        return carry
    lax.fori_loop(0, n_ck, score_chunk, 0)

    def count_over(pred):
        def run(cand_b):
            def body(c, cnt):
                k0 = pl.multiple_of(c * ck, ck)
                for jj in range(nsub):
                    kk = keys_scr[:, pl.ds(k0 + jj * LANES, LANES)]
                    kidx = k0 + jj * LANES + lane
                    cnt = cnt + jnp.where(pred(kk, kidx, cand_b), 1, 0)
                return cnt
            cnt = lax.fori_loop(0, n_ck, body, jnp.zeros((tq, LANES), I32))
            return jnp.sum(cnt, axis=1, keepdims=True)
        return run

    thr = _kth_largest(count_over(lambda kk, kidx, cb: kk >= cb), tq, topk)
    thr_b = _lane_bcast(thr)
    n_gt = count_over(lambda kk, kidx, cb: kk > cb)(thr_b)
    n_eq = count_over(lambda kk, kidx, cb: kk == cb)(thr_b)
    need = (thr > INT_MIN) & (n_gt + n_eq > topk)
    cut_scr[...] = _lane_bcast(jnp.where(thr == INT_MIN, -1, INT_BIG))

    @pl.when(jnp.max(need.astype(I32)) > 0)
    def _():
        cut = _tie_cut(count_over(lambda kk, kidx, cb: (kk == thr_b) & (kidx < cb)), need, topk - n_gt, tq, nbits)
        cut_scr[...] = _lane_bcast(jnp.where(thr == INT_MIN, -1, cut))
    cut_b = cut_scr[...]

    def selected(k0, jj):
        kk = keys_scr[:, pl.ds(k0 + jj * LANES, LANES)]
        kidx = k0 + jj * LANES + lane
        return (kk > thr_b) | ((kk == thr_b) & (kidx <= cut_b))

    def chunk_scores(k0, g):
        q2 = jnp.concatenate([q_ref[rep * g + r] for r in range(rep)], axis=0)
        return _dot(q2, kt_ref[HEAD_DIM * g:HEAD_DIM * (g + 1), pl.ds(k0, ck)])

    m_scr[...] = jnp.full(m_scr.shape, -jnp.inf, F32)

    def max_chunk(c, carry):
        k0 = pl.multiple_of(c * ck, ck)
        sels = [selected(k0, jj) for jj in range(nsub)]
        for g in range(N_KV_HEADS):
            s = chunk_scores(k0, g)
            m = m_scr[g]
            for jj in range(nsub):
                sel = jnp.concatenate([sels[jj]] * rep, axis=0)
                m = jnp.maximum(m, jnp.where(sel, s[:, jj * LANES:(jj + 1) * LANES], -jnp.inf))
            m_scr[g] = m
        return carry
    lax.fori_loop(0, n_ck, max_chunk, 0)
    for g in range(N_KV_HEADS):
        m_scr[g] = _lane_bcast(jnp.max(m_scr[g], axis=1, keepdims=True))

    l_scr[...] = jnp.zeros(l_scr.shape, F32)
    acc_scr[...] = jnp.zeros(acc_scr.shape, F32)

    def pv_chunk(c, carry):
        k0 = pl.multiple_of(c * ck, ck)
        sels = [selected(k0, jj) for jj in range(nsub)]
        for g in range(N_KV_HEADS):
            s = chunk_scores(k0, g)
            m = m_scr[g]
            l = l_scr[g]
            ps = []
            for jj in range(nsub):
                sel = jnp.concatenate([sels[jj]] * rep, axis=0)
                p = jnp.where(sel, jnp.exp(s[:, jj * LANES:(jj + 1) * LANES] - m), 0.0)
                l = l + p
                ps.append(p.astype(BF16))
            l_scr[g] = l
            acc_scr[g] += _dot(jnp.concatenate(ps, axis=1), vb_ref[pl.ds(k0, ck), HEAD_DIM * g:HEAD_DIM * (g + 1)])
        return carry
    lax.fori_loop(0, n_ck, pv_chunk, 0)

    outs = []
    for g in range(N_KV_HEADS):
        o = acc_scr[g] / jnp.sum(l_scr[g], axis=1, keepdims=True)
        outs += [o[r * tq:(r + 1) * tq] for r in range(rep)]
    o_ref[...] = jnp.concatenate(outs, axis=1).astype(BF16)


def _attn_prompt(iq8, q8, iks, ikt, kt, vb, topk):
    n = iks.shape[0]
    tq = min(128, n)
    ck = min(256, n)
    rep = N_HEADS // N_KV_HEADS
    nbits = max(1, int(np.ceil(np.log2(n))))
    return pl.pallas_call(
        functools.partial(_attn_prompt_kernel, tq=tq, ck=ck, topk=topk, nbits=nbits),
        grid=(n // tq,),
        in_specs=[pl.BlockSpec((IDX_HEADS, tq, IDX_DIM), lambda i: (0, i, 0)),
                  pl.BlockSpec((N_HEADS, tq, HEAD_DIM), lambda i: (0, i, 0)),
                  pl.BlockSpec((tq, LANES), lambda i: (i, 0)),
                  _resident((IDX_DIM, n)), _resident((KV_W, n)), _resident((n, KV_W))],
        out_specs=pl.BlockSpec((tq, ATTN_W), lambda i: (i, 0)),
        out_shape=jax.ShapeDtypeStruct((n, ATTN_W), BF16),
        scratch_shapes=[pltpu.VMEM((tq, n), I32),
                        pltpu.VMEM((IDX_HEADS, tq, LANES), F32),
                        pltpu.VMEM((tq, LANES), I32),
                        pltpu.VMEM((N_KV_HEADS, rep * tq, LANES), F32),
                        pltpu.VMEM((N_KV_HEADS, rep * tq, LANES), F32),
                        pltpu.VMEM((N_KV_HEADS, rep * tq, HEAD_DIM), F32)],
        compiler_params=_params("arbitrary"),
        name="attn_prompt",
    )(iq8, q8, iks, ikt, kt, vb)


def _attn_sample_kernel(pt_ref, iq_ref, q_ref, iks_ref, kbn_ref, vbn_ref, cik_hbm, ck_hbm, cv_hbm, o_ref,
                        ikp, kp, vp, kpb, sems, keys_scr, *, n_pages, t_new, ck, topk, nbits):
    b = pl.program_id(0)
    past = n_pages * PAGE_SIZE
    rows = IDX_HEADS * t_new
    n_ck = past // ck
    nsub = ck // LANES
    ppc = ck // PAGE_SIZE

    def page_copy(src, dst, sem, p, page):
        return pltpu.make_async_copy(src.at[page], dst.at[p], sem)

    def start_page(p, carry):
        page = pt_ref[b, p]
        page_copy(cik_hbm, ikp, sems.at[0], p, page).start()
        page_copy(ck_hbm, kp, sems.at[1], p, page).start()
        page_copy(cv_hbm, vp, sems.at[2], p, page).start()
        return carry
    lax.fori_loop(0, n_pages, start_page, 0)

    def wait_pages(src, dst, sem):
        def body(p, carry):
            page_copy(src, dst, sem, p, 0).wait()
            return carry
        lax.fori_loop(0, n_pages, body, 0)

    iks = iks_ref[...]
    w = iks[:, IW_LANE:IW_LANE + IDX_HEADS] * (IDX_HEADS ** -0.5)
    wb = _lane_bcast(jnp.concatenate([w[:, h:h + 1] for h in range(IDX_HEADS)], axis=0))
    iq = iq_ref[0].reshape(rows, IDX_DIM)
    qrow = lax.broadcasted_iota(I32, (t_new, LANES), 0)
    lane = lax.broadcasted_iota(I32, (t_new, LANES), 1)

    def head_sum(t):
        acc = t[0:t_new]
        for h in range(1, IDX_HEADS):
            acc = acc + t[h * t_new:(h + 1) * t_new]
        return acc

    wait_pages(cik_hbm, ikp, sems.at[0])

    def score_chunk(c, carry):
        k0 = pl.multiple_of(c * ck, ck)
        ikc = ikp[pl.ds(c * ppc, ppc)].reshape(ck, IDX_DIM).astype(BF16)
        s = _dot_t(iq, ikc)
        for jj in range(nsub):
            t = jnp.maximum(s[:, jj * LANES:(jj + 1) * LANES], 0.0) * wb
            keys_scr[:, pl.ds(k0 + jj * LANES, LANES)] = _sort_key(head_sum(t))
        return carry
    lax.fori_loop(0, n_ck, score_chunk, 0)

    pad_rows = lambda a: jnp.concatenate([a, jnp.zeros((LANES - t_new, a.shape[1]), a.dtype)], axis=0)
    new_ok = (lane <= qrow) & (lane < t_new)
    ikn = pad_rows(iks[:, :IDX_DIM].astype(BF16))
    s_new = jnp.maximum(_dot_t(iq, ikn), 0.0) * wb
    keys_scr[:, past:past + LANES] = jnp.where(new_ok, _sort_key(head_sum(s_new)), INT_MIN)

    n_tiles = past // LANES + 1
    tile_idx = lambda jt: jt * LANES + lane

    def count_over(pred):
        def run(cand_b):
            cnt = jnp.zeros((t_new, LANES), I32)
            for jt in range(n_tiles):
                cnt = cnt + jnp.where(pred(keys_scr[:, jt * LANES:(jt + 1) * LANES], tile_idx(jt), cand_b), 1, 0)
            return jnp.sum(cnt, axis=1, keepdims=True)
        return run

    thr = _kth_largest(count_over(lambda kk, kidx, cb: kk >= cb), t_new, topk)
    thr_b = _lane_bcast(thr)
    n_gt = count_over(lambda kk, kidx, cb: kk > cb)(thr_b)
    n_eq = count_over(lambda kk, kidx, cb: kk == cb)(thr_b)
    need = (thr > INT_MIN) & (n_gt + n_eq > topk)
    cut = _tie_cut(count_over(lambda kk, kidx, cb: (kk == thr_b) & (kidx < cb)), need, topk - n_gt, t_new, nbits)
    cut_b = _lane_bcast(jnp.where(thr == INT_MIN, -1, cut))

    def selected(jt):
        kk = keys_scr[:, jt * LANES:(jt + 1) * LANES]
        sel = (kk > thr_b) | ((kk == thr_b) & (tile_idx(jt) <= cut_b))
        return jnp.concatenate([sel] * N_HEADS, axis=0)

    wait_pages(ck_hbm, kp, sems.at[1])
    wait_pages(cv_hbm, vp, sems.at[2])
    q = q_ref[0].reshape(rows, KV_W)
    s_new = _dot_t(q, pad_rows(kbn_ref[...]))
    sel_new = selected(past // LANES)
    m0 = jnp.where(sel_new, s_new, -jnp.inf)

    def max_chunk(c, m):
        kc = kp[pl.ds(c * ppc, ppc)].reshape(ck, KV_W).astype(BF16)
        kpb[pl.ds(pl.multiple_of(c * ck, ck), ck), :] = kc
        s = _dot_t(q, kc)
        for jj in range(nsub):
            kk = keys_scr[:, pl.ds(pl.multiple_of(c * ck, ck) + jj * LANES, LANES)]
            kidx = c * ck + jj * LANES + lane
            sel = (kk > thr_b) | ((kk == thr_b) & (kidx <= cut_b))
            sel = jnp.concatenate([sel] * N_HEADS, axis=0)
            m = jnp.maximum(m, jnp.where(sel, s[:, jj * LANES:(jj + 1) * LANES], -jnp.inf))
        return m
    m = lax.fori_loop(0, n_ck, max_chunk, m0)
    m_b = _lane_bcast(jnp.max(m, axis=1, keepdims=True))

    p_new = jnp.where(sel_new, jnp.exp(s_new - m_b), 0.0)
    acc0 = _dot(p_new.astype(BF16), pad_rows(vbn_ref[...]))

    def pv_chunk(c, carry):
        l, acc = carry
        k0 = pl.multiple_of(c * ck, ck)
        s = _dot_t(q, kpb[pl.ds(k0, ck), :])
        ps = []
        for jj in range(nsub):
            kk = keys_scr[:, pl.ds(k0 + jj * LANES, LANES)]
            kidx = c * ck + jj * LANES + lane
            sel = (kk > thr_b) | ((kk == thr_b) & (kidx <= cut_b))
            sel = jnp.concatenate([sel] * N_HEADS, axis=0)
            p = jnp.where(sel, jnp.exp(s[:, jj * LANES:(jj + 1) * LANES] - m_b), 0.0)
            l = l + p
            ps.append(p.astype(BF16))
        vc = vp[pl.ds(c * ppc, ppc)].reshape(ck, KV_W).astype(BF16)
        return l, acc + _dot(jnp.concatenate(ps, axis=1), vc)
    l, acc = lax.fori_loop(0, n_ck, pv_chunk, (p_new, acc0))
    o = acc / jnp.sum(l, axis=1, keepdims=True)
    o_ref[0] = o.reshape(N_HEADS, t_new, KV_W).astype(BF16)


def _attn_sample(page_table, iq_s, q_s, iks_s, kb_s, vb_s, cache_idx_k, cache_k, cache_v, topk):
    db, n_pages = page_table.shape
    t_new = iq_s.shape[2]
    past = n_pages * PAGE_SIZE
    ck = min(512, past)
    rows = IDX_HEADS * t_new
    nbits = max(1, int(np.ceil(np.log2(past + LANES))))
    n_pool = cache_k.shape[0]
    grid_spec = pltpu.PrefetchScalarGridSpec(
        num_scalar_prefetch=1,
        grid=(db,),
        in_specs=[pl.BlockSpec((1, IDX_HEADS, t_new, IDX_DIM), lambda b, pt: (b, 0, 0, 0)),
                  pl.BlockSpec((1, N_HEADS, t_new, KV_W), lambda b, pt: (b, 0, 0, 0)),
                  pl.BlockSpec((t_new, LANES), lambda b, pt: (b, 0)),
                  pl.BlockSpec((t_new, KV_W), lambda b, pt: (b, 0)),
                  pl.BlockSpec((t_new, KV_W), lambda b, pt: (b, 0)),
                  pl.BlockSpec(memory_space=pl.ANY), pl.BlockSpec(memory_space=pl.ANY),
                  pl.BlockSpec(memory_space=pl.ANY)],
        out_specs=pl.BlockSpec((1, N_HEADS, t_new, KV_W), lambda b, pt: (b, 0, 0, 0)),
        scratch_shapes=[pltpu.VMEM((n_pages, PAGE_SIZE, IDX_DIM), F32),
                        pltpu.VMEM((n_pages, PAGE_SIZE, KV_W), F32),
                        pltpu.VMEM((n_pages, PAGE_SIZE, KV_W), F32),
                        pltpu.VMEM((past, KV_W), BF16),
                        pltpu.SemaphoreType.DMA((3,)),
                        pltpu.VMEM((t_new, past + LANES), I32)])
    return pl.pallas_call(
        functools.partial(_attn_sample_kernel, n_pages=n_pages, t_new=t_new, ck=ck, topk=topk, nbits=nbits),
        grid_spec=grid_spec,
        out_shape=jax.ShapeDtypeStruct((db, N_HEADS, t_new, KV_W), BF16),
        compiler_params=_params("arbitrary"),
        name="attn_sample",
    )(page_table, iq_s, q_s, iks_s, kb_s, vb_s,
      cache_idx_k.reshape(n_pool, PAGE_SIZE, IDX_DIM), cache_k.reshape(n_pool, PAGE_SIZE, KV_W),
      cache_v.reshape(n_pool, PAGE_SIZE, KV_W))


def _mix_kernel(attn_ref, sga_ref, yc_ref, x_ref, g1_ref, sh2_ref, sc2_ref, n2_ref, woa_ref, wout_ref,
                wr_hi_ref, wr_lo_ref, br_ref, x1_ref, h2_ref, comb_ref, *, sample, tm):
    if sample:
        a = attn_ref[...]
        y_attn = jnp.zeros((tm, D_MODEL), F32)
        for h in range(N_HEADS):
            y_attn = y_attn + _dot(a[:, h].reshape(tm, KV_W), woa_ref[h])
    else:
        y_attn = _dot(attn_ref[...], woa_ref[...])
    mixed = _dot((sga_ref[...] * y_attn + yc_ref[...]).astype(BF16), wout_ref[...])
    x3 = x_ref[...]
    g, r, _ = x3.shape
    x1 = x3 + g1_ref[...] * mixed.reshape(g, r, D_MODEL)
    x1_ref[...] = x1
    h2 = _modnorm(x1, n2_ref[...], sc2_ref[...], sh2_ref[...])
    h2_ref[...] = h2.astype(BF16)

    lg = _dot3(h2, wr_hi_ref[...], wr_lo_ref[...]) + br_ref[...]
    lane = lax.broadcasted_iota(I32, (tm, LANES), 1)
    is_g = lane < N_GROUPS
    is_e = (lane >= ROUTER_E0) & (lane < ROUTER_E0 + N_EXPERTS)
    rmax = lambda v: jnp.max(v, axis=1, keepdims=True)
    first = lambda msk: jnp.min(jnp.where(msk, lane, LANES), axis=1, keepdims=True)
    gl = jnp.where(is_g, lg, -jnp.inf)
    gmax = rmax(gl)
    g_sel = first(gl == gmax)
    g_w = 1.0 / jnp.sum(jnp.where(is_g, jnp.exp(lg - gmax), 0.0), axis=1, keepdims=True)
    in_grp = is_e & (((lane - ROUTER_E0) >> 2) == g_sel)
    me = jnp.where(in_grp, lg, -jnp.inf)
    v1 = rmax(me)
    i1 = first(in_grp & (me == v1))
    rest = in_grp & (lane != i1)
    me2 = jnp.where(rest, lg, -jnp.inf)
    v2 = rmax(me2)
    i2 = first(rest & (me2 == v2))
    e2 = jnp.exp(v2 - v1)
    den = 1.0 + e2
    comb_ref[...] = jnp.where(lane == i1, g_w / den, 0.0) + jnp.where(lane == i2, g_w * e2 / den, 0.0)


def _mix(attn, sga, yc, x3, mod3, mod_row0, norm2_g, w_oa, w_out, wr_hi, wr_lo, b_r, *, sample):
    G, R, _ = x3.shape
    n = G * R
    if sample:
        bb = min(G, 32)
        tm = bb * R
        nb = G // bb
        xspec = pl.BlockSpec((bb, R, D_MODEL), lambda i: (i, 0, 0))
        mspec = lambda c: pl.BlockSpec((bb, 1, D_MODEL), lambda i: (i, 0, c))
        aspec = pl.BlockSpec((bb, N_HEADS, R, KV_W), lambda i: (i, 0, 0, 0))
        woa_spec = _resident((N_HEADS, KV_W, D_MODEL))
    else:
        tm = min(R, 256)
        nb = R // tm
        xspec = pl.BlockSpec((1, tm, D_MODEL), lambda i: (0, i, 0))
        mspec = lambda c: pl.BlockSpec((1, 1, D_MODEL), lambda i: (mod_row0, 0, c))
        aspec = pl.BlockSpec((tm, ATTN_W), lambda i: (i, 0))
        woa_spec = _resident((ATTN_W, D_MODEL))
    row = lambda w: pl.BlockSpec((tm, w), lambda i: (i, 0))
    return pl.pallas_call(
        functools.partial(_mix_kernel, sample=sample, tm=tm),
        grid=(nb,),
        in_specs=[aspec, row(D_MODEL), row(D_MODEL), xspec, mspec(2), mspec(3), mspec(4), _resident((1, D_MODEL)),
                  woa_spec, _resident((D_MODEL, D_MODEL)), _resident((D_MODEL, LANES)), _resident((D_MODEL, LANES)),
                  _resident((1, LANES))],
        out_specs=[xspec, row(D_MODEL), row(LANES)],
        out_shape=[jax.ShapeDtypeStruct((G, R, D_MODEL), F32), jax.ShapeDtypeStruct((n, D_MODEL), BF16),
                   jax.ShapeDtypeStruct((n, LANES), F32)],
        compiler_params=_params("arbitrary"),
        name="mix_sample" if sample else "mix_prompt",
    )(attn, sga, yc, x3, mod3, mod3, mod3, norm2_g.reshape(1, D_MODEL), w_oa, w_out, wr_hi, wr_lo, b_r)


def _moe_kernel(h2_ref, comb_ref, x1_ref, g2_ref, fg_ref, weg_ref, weu_ref, wed_ref, y_ref, acc_scr, *, tm, epb):
    j = pl.program_id(1)

    @pl.when(j == 0)
    def _():
        acc_scr[...] = jnp.zeros(acc_scr.shape, F32)

    h2 = h2_ref[...]
    a = _dot(h2, weg_ref[...])
    u = _dot(h2, weu_ref[...])
    hid = a * _sigmoid(a) * u
    comb = comb_ref[...]
    lane = lax.broadcasted_iota(I32, (tm, LANES), 1)
    parts = []
    for e in range(epb):
        ce = jnp.sum(jnp.where(lane == ROUTER_E0 + j * epb + e, comb, 0.0), axis=1, keepdims=True)
        parts.append((hid[:, e * EXPERT_FF:(e + 1) * EXPERT_FF] * ce).astype(BF16))
    acc_scr[...] += _dot(jnp.concatenate(parts, axis=1), wed_ref[...])

    @pl.when(j == pl.num_programs(1) - 1)
    def _():
        x1 = x1_ref[...]
        g, r, _ = x1.shape
        x2 = x1 + g2_ref[...] * acc_scr[...].reshape(g, r, D_MODEL)
        ms = jnp.mean(x2 * x2, axis=-1, keepdims=True)
        y_ref[...] = x2 * lax.rsqrt(ms + EPS) * fg_ref[...][None]


def _moe(h2, comb, x1, mod3, mod_row0, final_g, w_eg, w_eu, w_ed, *, sample):
    G, R, _ = x1.shape
    epb = EXP_PER_GROUP
    nj = N_EXPERTS // epb
    wblk = epb * EXPERT_FF
    if sample:
        bb = min(G, 64)
        tm = bb * R
        nb = G // bb
        xspec = pl.BlockSpec((bb, R, D_MODEL), lambda i, j: (i, 0, 0))
        g2spec = pl.BlockSpec((bb, 1, D_MODEL), lambda i, j: (i, 0, 5))
    else:
        tm = min(R, 512)
        nb = R // tm
        xspec = pl.BlockSpec((1, tm, D_MODEL), lambda i, j: (0, i, 0))
        g2spec = pl.BlockSpec((1, 1, D_MODEL), lambda i, j: (mod_row0, 0, 5))
    return pl.pallas_call(
        functools.partial(_moe_kernel, tm=tm, epb=epb),
        grid=(nb, nj),
        in_specs=[pl.BlockSpec((tm, D_MODEL), lambda i, j: (i, 0)),
                  pl.BlockSpec((tm, LANES), lambda i, j: (i, 0)),
                  xspec, g2spec,
                  pl.BlockSpec((1, D_MODEL), lambda i, j: (0, 0)),
                  pl.BlockSpec((D_MODEL, wblk), lambda i, j: (0, j)),
                  pl.BlockSpec((D_MODEL, wblk), lambda i, j: (0, j)),
                  pl.BlockSpec((wblk, D_MODEL), lambda i, j: (j, 0))],
        out_specs=xspec,
        out_shape=jax.ShapeDtypeStruct((G, R, D_MODEL), F32),
        scratch_shapes=[pltpu.VMEM((tm, D_MODEL), F32)],
        compiler_params=_params("arbitrary", "arbitrary"),
        name="moe_sample" if sample else "moe_prompt",
    )(h2, comb, x1, mod3, final_g.reshape(1, D_MODEL), w_eg, w_eu, w_ed)


def kernel(x_prompt, x_sample, cache_k, cache_v, cache_idx_k, state_conv, page_table, c_prompt, c_sample,
           w_ada, b_ada, norm1_g, norm2_g, w_in_mix, conv_w, w_o_attn, w_o_conv, w_out,
           w_router_group, b_router_group, w_router_expert, b_router_expert,
           w_exp_gate, w_exp_up, w_exp_down, final_norm_g):
    depth = w_ada.shape[0]
    assert depth == 1, "single-layer step"
    B, S, _ = x_prompt.shape
    assert B == 1, "one prompt sequence"
    DB, T, _ = x_sample.shape
    assert T == SUBLANES, "sample tokens per request must fill one sublane tile"
    n_pages = page_table.shape[1]
    past = n_pages * PAGE_SIZE

    n_rows = DB + B
    pad = (-n_rows) % SUBLANES
    c_all = jnp.concatenate([c_sample, c_prompt, jnp.zeros((pad, D_MODEL), F32)], axis=0)
    mod = _ada(c_all, w_ada[0], b_ada[0])
    mod3 = mod.reshape(n_rows + pad, 1, 6 * D_MODEL)

    half = HEAD_DIM // 8
    inv_freq = ROPE_THETA ** (-jnp.arange(half, dtype=F32) / half)
    lane = np.arange(LANES)
    invf = jnp.where((lane % HEAD_DIM) < 2 * half, inv_freq[lane % half], 0.0).reshape(1, LANES).astype(F32)
    w_in = w_in_mix[0]
    w_oc = w_o_conv[0].astype(BF16)
    w_oa = w_o_attn[0]
    w_oa_p = w_oa.astype(BF16)
    rep = N_HEADS // N_KV_HEADS
    w_oa_s = jnp.stack([
        jnp.concatenate([jnp.zeros((HEAD_DIM * (h // rep), D_MODEL), F32), w_oa[HEAD_DIM * h:HEAD_DIM * (h + 1)],
                         jnp.zeros((KV_W - HEAD_DIM * (h // rep + 1), D_MODEL), F32)], axis=0)
        for h in range(N_HEADS)]).astype(BF16)
    w_o = w_out[0].astype(BF16)
    zr = jnp.zeros((D_MODEL, LANES - ROUTER_E0 - N_EXPERTS), F32)
    w_r = jnp.concatenate([w_router_group[0], w_router_expert[0], zr], axis=1)
    wr_hi = w_r.astype(BF16)
    wr_lo = (w_r - wr_hi.astype(F32)).astype(BF16)
    b_r = jnp.concatenate([b_router_group[0], b_router_expert[0],
                           jnp.zeros((LANES - ROUTER_E0 - N_EXPERTS,), F32)]).reshape(1, LANES)
    cat_e = lambda w: jnp.transpose(w, (1, 0, 2)).reshape(D_MODEL, N_EXPERTS * EXPERT_FF).astype(BF16)
    w_eg = cat_e(w_exp_gate[0])
    w_eu = cat_e(w_exp_up[0])
    w_ed = w_exp_down[0].reshape(N_EXPERTS * EXPERT_FF, D_MODEL).astype(BF16)

    xp3 = x_prompt
    (q8, iq8, iks_p, k_p, v_p, kt, ikt, vb_p, yc_p, sga_p, cst_p) = _proj(
        xp3, mod3, DB, norm1_g[0], _proj_weight(w_in, False), invf, conv_w[0], w_oc, None, sample=False, pos_base=0)
    attn_p = _attn_prompt(iq8, q8, iks_p, ikt, kt, vb_p, min(TOPK_MAX, S // 4))
    x1_p, h2_p, comb_p = _mix(attn_p, sga_p, yc_p, xp3, mod3, DB, norm2_g[0], w_oa_p, w_o, wr_hi, wr_lo, b_r,
                              sample=False)
    y_p = _moe(h2_p, comb_p, x1_p, mod3, DB, final_norm_g, w_eg, w_eu, w_ed, sample=False)

    (q_s, iq_s, iks_s, k_s, v_s, kb_s, vb_s, yc_s, sga_s, cst_s) = _proj(
        x_sample, mod3, 0, norm1_g[0], _proj_weight(w_in, True), invf, conv_w[0], w_oc, state_conv[0],
        sample=True, pos_base=past)
    attn_s = _attn_sample(page_table, iq_s, q_s, iks_s, kb_s, vb_s, cache_idx_k[0], cache_k[0], cache_v[0],
                          min(TOPK_MAX, (past + T) // 4))
    x1_s, h2_s, comb_s = _mix(attn_s, sga_s, yc_s, x_sample, mod3, 0, norm2_g[0], w_oa_s, w_o, wr_hi, wr_lo, b_r,
                              sample=True)
    y_s = _moe(h2_s, comb_s, x1_s, mod3, 0, final_norm_g, w_eg, w_eu, w_ed, sample=True)

    return (y_p, y_s,
            k_p.reshape(1, B, S, N_KV_HEADS, HEAD_DIM), v_p.reshape(1, B, S, N_KV_HEADS, HEAD_DIM),
            iks_p[:, :IDX_DIM].reshape(1, B, S, IDX_DIM),
            cst_p[SUBLANES - (CONV_K - 1):].reshape(1, B, CONV_K - 1, CONV_W),
            k_s.reshape(1, DB, T, N_KV_HEADS, HEAD_DIM), v_s.reshape(1, DB, T, N_KV_HEADS, HEAD_DIM),
            iks_s[:, :IDX_DIM].reshape(1, DB, T, IDX_DIM),
            cst_s[:, T - (CONV_K - 1):].reshape(1, DB, CONV_K - 1, CONV_W))
```

```python
import functools

import jax
import jax.numpy as jnp
import numpy as np
from jax import lax
from jax.experimental import pallas as pl
from jax.experimental.pallas import tpu as pltpu

F32 = jnp.float32
BF16 = jnp.bfloat16
I32 = jnp.int32

D_MODEL = 1024
N_HEADS = 8
N_KV_HEADS = 4
HEAD_DIM = 64
ATTN_W = N_HEADS * HEAD_DIM
KV_W = N_KV_HEADS * HEAD_DIM
ROPE_THETA = 500000.0
IDX_HEADS = 8
IDX_DIM = 64
TOPK_MAX = 256
CONV_W = 512
CONV_K = 3
N_GROUPS = 4
EXP_PER_GROUP = 4
N_EXPERTS = N_GROUPS * EXP_PER_GROUP
EXPERT_FF = 256
EPS = 1e-6
PAGE_SIZE = 128

LANES = 128
SUBLANES = 8
VMEM_LIMIT = 56 * 1024 * 1024
INT_BIG = 2 ** 30
BIG_SCORE = 3.0e38
BISECT_STEPS = 40
LOG2E = 1.4426950408889634
Q_SCALE = HEAD_DIM ** -0.5 * LOG2E
IW_LANE = 112
ROUTER_E0 = 4
REP = N_HEADS // N_KV_HEADS


def _dot(a, b):
    return jnp.dot(a, b, preferred_element_type=F32)


def _dot_t(a, b):
    return lax.dot_general(a, b, (((1,), (1,)), ((), ())), preferred_element_type=F32)


def _split_bf16(a):
    hi = a.astype(BF16)
    lo = (a - hi.astype(F32)).astype(BF16)
    return hi, lo


def _dot3(a, w_hi, w_lo):
    a_hi, a_lo = _split_bf16(a)
    return _dot(a_hi, w_hi) + (_dot(a_hi, w_lo) + _dot(a_lo, w_hi))


def _sigmoid(x):
    return 1.0 / (1.0 + jnp.exp(-x))


def _params(*sem):
    return pltpu.CompilerParams(dimension_semantics=sem, vmem_limit_bytes=VMEM_LIMIT)


def _resident(shape):
    nd = len(shape)
    return pl.BlockSpec(shape, lambda *_: (0,) * nd, pipeline_mode=pl.Buffered(1))


def _ada_kernel(c_ref, w_ref, b_ref, o_ref):
    c = c_ref[...]
    a = c * _sigmoid(c)
    w = w_ref[...]
    w_hi, w_lo = _split_bf16(w)
    o_ref[...] = _dot3(a, w_hi, w_lo) + b_ref[...]


def _ada(c_all, w_ada, b_ada):
    rows = c_all.shape[0]
    n = w_ada.shape[1]
    tn = 512
    return pl.pallas_call(
        _ada_kernel,
        grid=(n // tn,),
        in_specs=[pl.BlockSpec((rows, D_MODEL), lambda j: (0, 0)),
                  pl.BlockSpec((D_MODEL, tn), lambda j: (0, j)),
                  pl.BlockSpec((1, tn), lambda j: (0, j))],
        out_specs=pl.BlockSpec((rows, tn), lambda j: (0, j)),
        out_shape=jax.ShapeDtypeStruct((rows, n), F32),
        compiler_params=_params("arbitrary"),
        name="ada",
    )(c_all, w_ada, b_ada.reshape(1, n))


def _modnorm(x3, g, sc3, sh3):
    ms = jnp.mean(x3 * x3, axis=-1, keepdims=True)
    xn = x3 * lax.rsqrt(ms + EPS)
    h = xn * g[None] * (1.0 + sc3) + sh3
    return h.reshape(x3.shape[0] * x3.shape[1], x3.shape[2])


def _proj_kernel(*refs, sample, tm, pos_base, cols):
    if sample:
        (x_ref, sh_ref, sc_ref, g_ref, w_ref, invf_ref, cw_ref, woc_ref, st_ref,
         qo_ref, iqo_ref, iks_ref, k_ref, v_ref, kb_ref, vb_ref, yc_ref, sga_ref, cst_ref) = refs
    else:
        (x_ref, sh_ref, sc_ref, g_ref, w_ref, invf_ref, cw_ref, woc_ref,
         qo_ref, iqo_ref, iks_ref, k_ref, v_ref, kt_ref, ikt_ref, vb_ref, yc_ref, sga_ref, cst_ref,
         cu_scr) = refs
    i = pl.program_id(0)
    hb = _modnorm(x_ref[...], g_ref[...], sc_ref[...], sh_ref[...]).astype(BF16)

    def proj(name):
        c0, c1 = cols[name]
        return _dot(hb, w_ref[:, c0:c1])

    lane = lax.broadcasted_iota(I32, (tm, LANES), 1)
    row = lax.broadcasted_iota(I32, (tm, LANES), 0)
    if sample:
        pos = pos_base + (row & (SUBLANES - 1))
    else:
        pos = i * tm + row
    ang = pos.astype(F32) * invf_ref[...]
    cosv = jnp.cos(ang)
    sinv = jnp.sin(ang)
    j = lane & (HEAD_DIM - 1)
    half = HEAD_DIM // 8
    rc = jnp.where(j < 2 * half, cosv, 1.0)
    rs_lo = jnp.where((j >= half) & (j < 2 * half), sinv, 0.0)
    rs_hi = jnp.where(j < half, -sinv, 0.0)

    def rope(xs):
        return xs * rc + pltpu.roll(xs, half, 1) * rs_lo + pltpu.roll(xs, LANES - half, 1) * rs_hi

    pq = proj("q")
    if sample:
        for h in range(N_HEADS):
            blk = pq[:, KV_W * h:KV_W * (h + 1)]
            r = jnp.concatenate([rope(blk[:, :LANES]), rope(blk[:, LANES:])], axis=1) * Q_SCALE
            qo_ref[:, h] = r.reshape(tm // SUBLANES, SUBLANES, KV_W).astype(BF16)
    else:
        for s in range(ATTN_W // LANES):
            r = (rope(pq[:, LANES * s:LANES * (s + 1)]) * Q_SCALE).astype(BF16)
            qo_ref[2 * s] = r[:, :HEAD_DIM]
            qo_ref[2 * s + 1] = r[:, HEAD_DIM:]

    pk = proj("k")
    for s in range(KV_W // LANES):
        rk = rope(pk[:, LANES * s:LANES * (s + 1)])
        k_ref[:, LANES * s:LANES * (s + 1)] = rk
        if sample:
            kb_ref[:, LANES * s:LANES * (s + 1)] = rk.astype(BF16)
        else:
            kt_ref[LANES * s:LANES * (s + 1), :] = rk.T.astype(BF16)

    pv = proj("v")
    v_ref[...] = pv
    if sample:
        vb_ref[...] = pv.astype(BF16)
    else:
        tail = jnp.where(lane == HEAD_DIM, 1.0, 0.0)
        for s in range(KV_W // LANES):
            slab = pv[:, LANES * s:LANES * (s + 1)]
            vb_ref[:, LANES * 2 * s:LANES * (2 * s + 1)] = jnp.where(lane < HEAD_DIM, slab, tail).astype(BF16)
            vb_ref[:, LANES * (2 * s + 1):LANES * (2 * s + 2)] = jnp.where(
                lane < HEAD_DIM, pltpu.roll(slab, HEAD_DIM, 1), tail).astype(BF16)

    piq = proj("iq")
    for s in range(IDX_HEADS * IDX_DIM // LANES):
        r = (rope(piq[:, LANES * s:LANES * (s + 1)]) * (IDX_DIM ** -0.5)).astype(BF16)
        if sample:
            iqo_ref[:, 2 * s] = r[:, :IDX_DIM].reshape(tm // SUBLANES, SUBLANES, IDX_DIM)
            iqo_ref[:, 2 * s + 1] = r[:, IDX_DIM:].reshape(tm // SUBLANES, SUBLANES, IDX_DIM)
        else:
            iqo_ref[2 * s] = r[:, :IDX_DIM]
            iqo_ref[2 * s + 1] = r[:, IDX_DIM:]

    slab = rope(proj("iks"))
    iks_ref[...] = slab
    if not sample:
        ikt_ref[...] = slab.T[:IDX_DIM, :].astype(BF16)

    cu = proj("gc") * proj("u")
    cw = cw_ref[...]
    if sample:
        g8 = tm // SUBLANES
        st = st_ref[...]
        grp = (g8, SUBLANES, CONV_W)
        s0 = jnp.broadcast_to(st[:, 0:1, :], grp).reshape(tm, CONV_W)
        s1 = jnp.broadcast_to(st[:, 1:2, :], grp).reshape(tm, CONV_W)
        t = lax.broadcasted_iota(I32, (tm, CONV_W), 0) & (SUBLANES - 1)
        m1 = jnp.where(t >= 1, pltpu.roll(cu, 1, 0), s1)
        m2 = jnp.where(t >= 2, pltpu.roll(cu, 2, 0), jnp.where(t == 1, s1, s0))
        conv = m2 * cw[0:1] + m1 * cw[1:2] + cu * cw[2:3]
        cst_ref[...] = cu.reshape(grp)
    else:
        @pl.when(i == 0)
        def _():
            cu_scr[0:SUBLANES, :] = jnp.zeros((SUBLANES, CONV_W), F32)
        cu_scr[SUBLANES:SUBLANES + tm, :] = cu
        m1 = cu_scr[SUBLANES - 1:SUBLANES - 1 + tm, :]
        m2 = cu_scr[SUBLANES - 2:SUBLANES - 2 + tm, :]
        conv = m2 * cw[0:1] + m1 * cw[1:2] + cu * cw[2:3]
        tail = cu_scr[tm:tm + SUBLANES, :]
        cu_scr[0:SUBLANES, :] = tail
        cst_ref[...] = tail
    zc = (proj("gb_conv") * conv).astype(BF16)
    yconv = _dot(zc, woc_ref[...])
    sga_ref[...] = _sigmoid(proj("ga"))
    yc_ref[...] = _sigmoid(proj("gb")) * yconv


def _proj_layout(sample):
    names = [("q", N_HEADS * KV_W if sample else ATTN_W), ("k", KV_W), ("v", KV_W),
             ("iq", IDX_HEADS * IDX_DIM), ("iks", LANES), ("u", CONV_W), ("gb_conv", CONV_W), ("gc", CONV_W),
             ("ga", D_MODEL), ("gb", D_MODEL)]
    cols, c = {}, 0
    for n, w in names:
        cols[n] = (c, c + w)
        c += w
    return cols, c


def _proj_weight(w_in, sample):
    splits = (ATTN_W, KV_W, KV_W, IDX_HEADS * IDX_DIM, IDX_DIM, IDX_HEADS, CONV_W, CONV_W, CONV_W, D_MODEL, D_MODEL)
    parts, c = [], 0
    for w in splits:
        parts.append(w_in[:, c:c + w])
        c += w
    wq, wk, wv, wiq, wik, wiw, wu, wgb, wgc, wga, wgbr = parts
    z = lambda n: jnp.zeros((D_MODEL, n), w_in.dtype)
    if sample:
        qparts = []
        for h in range(N_HEADS):
            g = h // REP
            qparts += [z(HEAD_DIM * g), wq[:, HEAD_DIM * h:HEAD_DIM * (h + 1)], z(KV_W - HEAD_DIM * (g + 1))]
        wq = jnp.concatenate(qparts, axis=1)
    slab = jnp.concatenate([wik, z(IW_LANE - IDX_DIM), wiw, z(LANES - IW_LANE - IDX_HEADS)], axis=1)
    return jnp.concatenate([wq, wk, wv, wiq, slab, wu, wgb, wgc, wga, wgbr], axis=1).astype(BF16)


def _proj(x3, mod3, mod_row0, norm_g, w_p, invf, conv_w, w_oc, state, *, sample, pos_base):
    G, R, _ = x3.shape
    n = G * R
    cols, wtot = _proj_layout(sample)
    if sample:
        bb = min(G, 64)
        tm = bb * R
        nb = G // bb
        xspec = pl.BlockSpec((bb, R, D_MODEL), lambda i: (i, 0, 0))
        shspec = pl.BlockSpec((bb, 1, D_MODEL), lambda i: (i, 0, 0))
        scspec = pl.BlockSpec((bb, 1, D_MODEL), lambda i: (i, 0, 1))
    else:
        tm = min(R, 512)
        nb = R // tm
        xspec = pl.BlockSpec((1, tm, D_MODEL), lambda i: (0, i, 0))
        shspec = pl.BlockSpec((1, 1, D_MODEL), lambda i: (mod_row0, 0, 0))
        scspec = pl.BlockSpec((1, 1, D_MODEL), lambda i: (mod_row0, 0, 1))
    row = lambda w: pl.BlockSpec((tm, w), lambda i: (i, 0))
    in_specs = [xspec, shspec, scspec, _resident((1, D_MODEL)), _resident((D_MODEL, wtot)), _resident((1, LANES)),
                _resident((CONV_K, CONV_W)), _resident((CONV_W, D_MODEL))]
    args = [x3, mod3, mod3, norm_g.reshape(1, D_MODEL), w_p, invf, conv_w, w_oc]
    if sample:
        in_specs.append(pl.BlockSpec((bb, CONV_K - 1, CONV_W), lambda i: (i, 0, 0)))
        args.append(state)
        out_specs = [pl.BlockSpec((bb, N_HEADS, R, KV_W), lambda i: (i, 0, 0, 0)),
                     pl.BlockSpec((bb, IDX_HEADS, R, IDX_DIM), lambda i: (i, 0, 0, 0)),
                     row(LANES), row(KV_W), row(KV_W), row(KV_W), row(KV_W), row(D_MODEL), row(D_MODEL),
                     pl.BlockSpec((bb, R, CONV_W), lambda i: (i, 0, 0))]
        out_shape = [jax.ShapeDtypeStruct((G, N_HEADS, R, KV_W), BF16),
                     jax.ShapeDtypeStruct((G, IDX_HEADS, R, IDX_DIM), BF16),
                     jax.ShapeDtypeStruct((n, LANES), F32),
                     jax.ShapeDtypeStruct((n, KV_W), F32), jax.ShapeDtypeStruct((n, KV_W), F32),
                     jax.ShapeDtypeStruct((n, KV_W), BF16), jax.ShapeDtypeStruct((n, KV_W), BF16),
                     jax.ShapeDtypeStruct((n, D_MODEL), F32), jax.ShapeDtypeStruct((n, D_MODEL), F32),
                     jax.ShapeDtypeStruct((G, R, CONV_W), F32)]
        scratch = []
    else:
        out_specs = [pl.BlockSpec((N_HEADS, tm, HEAD_DIM), lambda i: (0, i, 0)),
                     pl.BlockSpec((IDX_HEADS, tm, IDX_DIM), lambda i: (0, i, 0)),
                     row(LANES), row(KV_W), row(KV_W),
                     pl.BlockSpec((KV_W, tm), lambda i: (0, i)), pl.BlockSpec((IDX_DIM, tm), lambda i: (0, i)),
                     row(N_KV_HEADS * LANES), row(D_MODEL), row(D_MODEL),
                     pl.BlockSpec((SUBLANES, CONV_W), lambda i: (0, 0))]
        out_shape = [jax.ShapeDtypeStruct((N_HEADS, n, HEAD_DIM), BF16),
                     jax.ShapeDtypeStruct((IDX_HEADS, n, IDX_DIM), BF16),
                     jax.ShapeDtypeStruct((n, LANES), F32),
                     jax.ShapeDtypeStruct((n, KV_W), F32), jax.ShapeDtypeStruct((n, KV_W), F32),
                     jax.ShapeDtypeStruct((KV_W, n), BF16), jax.ShapeDtypeStruct((IDX_DIM, n), BF16),
                     jax.ShapeDtypeStruct((n, N_KV_HEADS * LANES), BF16),
                     jax.ShapeDtypeStruct((n, D_MODEL), F32), jax.ShapeDtypeStruct((n, D_MODEL), F32),
                     jax.ShapeDtypeStruct((SUBLANES, CONV_W), F32)]
        scratch = [pltpu.VMEM((tm + SUBLANES, CONV_W), F32)]
    return pl.pallas_call(
        functools.partial(_proj_kernel, sample=sample, tm=tm, pos_base=pos_base, cols=cols),
        grid=(nb,), in_specs=in_specs, out_specs=out_specs, out_shape=out_shape, scratch_shapes=scratch,
        compiler_params=_params("arbitrary"),
        name="proj_sample" if sample else "proj_prompt",
    )(*args)


def _lane_bcast(col):
    return jnp.broadcast_to(col, (col.shape[0], LANES))


def _lane_min(a):
    return jnp.min(a, axis=1, keepdims=True)


def _lane_max(a):
    return jnp.max(a, axis=1, keepdims=True)


def _lane_sum(a):
    return jnp.sum(a, axis=1, keepdims=True)


def _threshold_bracket(g_even, g_odd):
    lo = jnp.maximum(jnp.minimum(_lane_min(g_even), _lane_min(g_odd)), -BIG_SCORE)
    top = jnp.maximum(_lane_max(g_even), _lane_max(g_odd))
    return lo, top + jnp.maximum(jnp.abs(top) * 2.0 ** -10, 1e-30)


def _select_threshold(count_ge, max_below, lo, hi, done0, topk):
    rows = lo.shape[0]
    live0 = jnp.logical_not(done0)
    any_row = lambda msk: jnp.max(msk.astype(I32)) > 0

    def bisect_cond(st):
        _, _, c_lo, _, it = st
        return any_row(live0 & (c_lo != topk)) & (it < BISECT_STEPS)

    def bisect(st):
        lo, hi, c_lo, c_hi, it = st
        act = live0 & (c_lo != topk)
        mid = 0.5 * lo + 0.5 * hi
        c = count_ge(_lane_bcast(mid))
        up = act & (c >= topk)
        dn = act & (c < topk)
        return (jnp.where(up, mid, lo), jnp.where(dn, mid, hi), jnp.where(up, c, c_lo), jnp.where(dn, c, c_hi),
                it + 1)

    lo, hi, c_lo, c_hi, _ = lax.while_loop(
        bisect_cond, bisect,
        (lo, hi, jnp.full((rows, 1), INT_BIG, I32), jnp.zeros((rows, 1), I32), jnp.int32(0)))

    def snap_cond(st):
        return jnp.max(st[4]) > 0

    def snap(st):
        lo, hi, c_lo, c_hi, open_ = st
        cand = max_below(_lane_bcast(hi))
        c = count_ge(_lane_bcast(cand))
        found = (open_ > 0) & (c >= topk)
        down = (open_ > 0) & (c < topk)
        return (jnp.where(found, cand, lo), jnp.where(down, cand, hi), jnp.where(found, c, c_lo),
                jnp.where(down, c, c_hi), down.astype(I32))

    lo, hi, c_lo, c_hi, _ = lax.while_loop(
        snap_cond, snap, (lo, hi, c_lo, c_hi, (live0 & (c_lo != topk)).astype(I32)))
    tie = live0 & (c_lo != topk)
    return jnp.where(done0, -BIG_SCORE, lo), tie, topk - c_hi


def _tie_cut(count_tie_below, need, r, rows, nbits):
    def bit_step(b, x):
        cand = x + lax.shift_left(jnp.int32(1), nbits - 1 - b)
        cnt = count_tie_below(_lane_bcast(cand))
        return jnp.where(cnt < r, cand, x)
    x = lax.fori_loop(0, nbits, bit_step, jnp.zeros((rows, 1), I32))
    return jnp.where(need, x, INT_BIG)


def _selected(kk, kidx, thr_b, cut_b):
    return (kk > thr_b) | ((kk == thr_b) & (kidx <= cut_b))


def _attn_prompt_kernel(iq_ref, q_ref, iks_ref, ikt_ref, kt_ref, va_ref, o_ref,
                        score_scr, wb_scr, cut_scr, m_scr, qa_scr, acc_scr, *, tq, ck, topk, nbits):
    qi = pl.program_id(0)
    pair = 2 * ck
    n_pair = (qi * tq + tq + pair - 1) // pair
    nsub = ck // LANES

    w = iks_ref[:, IW_LANE:IW_LANE + IDX_HEADS] * (IDX_HEADS ** -0.5)
    for h in range(IDX_HEADS):
        wb_scr[h] = _lane_bcast(w[:, h:h + 1])
    qpos = qi * tq + lax.broadcasted_iota(I32, (tq, LANES), 0)
    lane = lax.broadcasted_iota(I32, (tq, LANES), 1)

    def for_chunks(fn, init):
        def body(c2, carry):
            for half in range(2):
                carry = fn(pl.multiple_of(c2 * pair + half * ck, ck), carry)
            return carry
        return lax.fori_loop(0, n_pair, body, init)

    def score_chunk(k0, gmax):
        ikc = ikt_ref[:, pl.ds(k0, ck)]
        acc = jnp.zeros((tq, ck), F32)
        for h in range(IDX_HEADS):
            s = _dot(iq_ref[h], ikc)
            wbh = wb_scr[h]
            acc = acc + jnp.maximum(s, 0.0) * jnp.concatenate([wbh] * nsub, axis=1)
        gmax = list(gmax)
        for jj in range(nsub):
            kidx = k0 + jj * LANES + lane
            kj = jnp.where(kidx <= qpos, acc[:, jj * LANES:(jj + 1) * LANES], -jnp.inf)
            score_scr[:, pl.ds(k0 + jj * LANES, LANES)] = kj
            gmax[jj % 2] = jnp.maximum(gmax[jj % 2], kj)
        return tuple(gmax)
    floor = jnp.full((tq, LANES), -jnp.inf, F32)
    g_even, g_odd = for_chunks(score_chunk, (floor, floor))

    def count_over(pred):
        def run(cand_b):
            def body(k0, cnt):
                cnt = list(cnt)
                for jj in range(nsub):
                    kk = score_scr[:, pl.ds(k0 + jj * LANES, LANES)]
                    kidx = k0 + jj * LANES + lane
                    cnt[jj % 2] = cnt[jj % 2] + jnp.where(pred(kk, kidx, cand_b), 1, 0)
                return tuple(cnt)
            zero = jnp.zeros((tq, LANES), I32)
            c0, c1 = for_chunks(body, (zero, zero))
            return _lane_sum(c0 + c1)
        return run

    def max_below(x_b):
        def body(k0, m):
            for jj in range(nsub):
                kk = score_scr[:, pl.ds(k0 + jj * LANES, LANES)]
                m = jnp.maximum(m, jnp.where(kk < x_b, kk, -jnp.inf))
            return m
        return _lane_max(for_chunks(body, floor))

    lo0, hi0 = _threshold_bracket(g_even, g_odd)
    done0 = qpos[:, 0:1] + 1 <= topk
    thr, tie, n_tie = _select_threshold(count_over(lambda kk, kidx, cb: kk >= cb), max_below, lo0, hi0, done0, topk)
    thr_b = _lane_bcast(thr)
    cut_scr[...] = jnp.full((tq, LANES), INT_BIG, I32)

    @pl.when(jnp.max(tie.astype(I32)) > 0)
    def _():
        cut = _tie_cut(count_over(lambda kk, kidx, cb: (kk == thr_b) & (kidx < cb)), tie, n_tie, tq, nbits)
        cut_scr[...] = _lane_bcast(cut)
    cut_b = cut_scr[...]

    def sel_tiles(k0):
        out = []
        for jj in range(nsub):
            kk = score_scr[:, pl.ds(k0 + jj * LANES, LANES)]
            sel = _selected(kk, k0 + jj * LANES + lane, thr_b, cut_b)
            out.append(jnp.concatenate([sel] * REP, axis=0))
        return out

    m_scr[...] = jnp.full(m_scr.shape, -jnp.inf, F32)

    def max_chunk(k0, carry):
        sels = sel_tiles(k0)
        for g in range(N_KV_HEADS):
            q2 = jnp.concatenate([q_ref[REP * g + r] for r in range(REP)], axis=0)
            s = _dot(q2, kt_ref[HEAD_DIM * g:HEAD_DIM * (g + 1), pl.ds(k0, ck)])
            m = m_scr[g]
            for jj in range(nsub):
                m = jnp.maximum(m, jnp.where(sels[jj], s[:, jj * LANES:(jj + 1) * LANES], -jnp.inf))
            m_scr[g] = m
        return carry
    for_chunks(max_chunk, 0)

    lane2 = lax.broadcasted_iota(I32, (REP * tq, LANES), 1)
    for g in range(N_KV_HEADS):
        q2 = jnp.concatenate([q_ref[REP * g + r] for r in range(REP)], axis=0).astype(F32)
        q2 = jnp.concatenate([q2, jnp.zeros((REP * tq, LANES - HEAD_DIM), F32)], axis=1)
        neg_m = -_lane_bcast(_lane_max(m_scr[g]))
        qa_scr[g] = jnp.where(lane2 == HEAD_DIM, neg_m, q2).astype(BF16)
    ones_row = jnp.where(lax.broadcasted_iota(I32, (LANES - HEAD_DIM, ck), 0) == 0, 1.0, 0.0).astype(BF16)

    acc_scr[...] = jnp.zeros(acc_scr.shape, F32)

    def pv_chunk(k0, carry):
        sels = sel_tiles(k0)
        for g in range(N_KV_HEADS):
            k_aug = jnp.concatenate([kt_ref[HEAD_DIM * g:HEAD_DIM * (g + 1), pl.ds(k0, ck)], ones_row], axis=0)
            s = _dot(qa_scr[g], k_aug)
            ps = [jnp.where(sels[jj], jnp.exp2(s[:, jj * LANES:(jj + 1) * LANES]), 0.0).astype(BF16)
                  for jj in range(nsub)]
            acc_scr[g] += _dot(jnp.concatenate(ps, axis=1), va_ref[pl.ds(k0, ck), LANES * g:LANES * (g + 1)])
        return carry
    for_chunks(pv_chunk, 0)

    outs = []
    for g in range(N_KV_HEADS):
        acc = acc_scr[g]
        o = acc[:, :HEAD_DIM] / acc[:, HEAD_DIM:HEAD_DIM + 1]
        outs += [o[r * tq:(r + 1) * tq] for r in range(REP)]
    o_ref[...] = jnp.concatenate(outs, axis=1).astype(BF16)


def _attn_prompt(iq8, q8, iks, ikt, kt, va, topk):
    n = iks.shape[0]
    tq = min(128, n)
    ck = min(256, n // 2)
    nbits = max(1, int(np.ceil(np.log2(n))))
    return pl.pallas_call(
        functools.partial(_attn_prompt_kernel, tq=tq, ck=ck, topk=topk, nbits=nbits),
        grid=(n // tq,),
        in_specs=[pl.BlockSpec((IDX_HEADS, tq, IDX_DIM), lambda i: (0, i, 0)),
                  pl.BlockSpec((N_HEADS, tq, HEAD_DIM), lambda i: (0, i, 0)),
                  pl.BlockSpec((tq, LANES), lambda i: (i, 0)),
                  _resident((IDX_DIM, n)), _resident((KV_W, n)), _resident((n, N_KV_HEADS * LANES))],
        out_specs=pl.BlockSpec((tq, ATTN_W), lambda i: (i, 0)),
        out_shape=jax.ShapeDtypeStruct((n, ATTN_W), BF16),
        scratch_shapes=[pltpu.VMEM((tq, n), F32),
                        pltpu.VMEM((IDX_HEADS, tq, LANES), F32),
                        pltpu.VMEM((tq, LANES), I32),
                        pltpu.VMEM((N_KV_HEADS, REP * tq, LANES), F32),
                        pltpu.VMEM((N_KV_HEADS, REP * tq, LANES), BF16),
                        pltpu.VMEM((N_KV_HEADS, REP * tq, LANES), F32)],
        compiler_params=_params("arbitrary"),
        name="attn_prompt",
    )(iq8, q8, iks, ikt, kt, va)


def _attn_sample_kernel(pt_ref, iq_ref, q_ref, iks_ref, kbn_ref, vbn_ref, cik_hbm, ck_hbm, cv_hbm, o_ref,
                        ikp, kp, vp, sems, score_scr, cut_scr, *, n_pages, t_new, ppc, topk, nbits):
    b = pl.program_id(0)
    past = n_pages * PAGE_SIZE
    rows = IDX_HEADS * t_new
    n_ck = n_pages // ppc
    ck = ppc * PAGE_SIZE

    def page_copy(src, dst, sem, p, page):
        return pltpu.make_async_copy(src.at[page], dst.at[p], sem)

    def start_page(p, carry):
        page = pt_ref[b, p]
        page_copy(cik_hbm, ikp, sems.at[0], p, page).start()
        page_copy(ck_hbm, kp, sems.at[1], p, page).start()
        page_copy(cv_hbm, vp, sems.at[2], p, page).start()
        return carry
    lax.fori_loop(0, n_pages, start_page, 0)

    def wait_pages(src, dst, sem):
        def body(p, carry):
            page_copy(src, dst, sem, p, 0).wait()
            return carry
        lax.fori_loop(0, n_pages, body, 0)

    def chunk_of(buf, c, width):
        return jnp.concatenate([buf[c * ppc + j].reshape(width, PAGE_SIZE) for j in range(ppc)], axis=1).astype(BF16)

    iks = iks_ref[...]
    w = iks[:, IW_LANE:IW_LANE + IDX_HEADS] * (IDX_HEADS ** -0.5)
    wb = _lane_bcast(jnp.concatenate([w[:, h:h + 1] for h in range(IDX_HEADS)], axis=0))
    iq = iq_ref[0].reshape(rows, IDX_DIM)
    qrow = lax.broadcasted_iota(I32, (t_new, LANES), 0)
    lane = lax.broadcasted_iota(I32, (t_new, LANES), 1)

    def head_sum(t):
        acc = t[0:t_new]
        for h in range(1, IDX_HEADS):
            acc = acc + t[h * t_new:(h + 1) * t_new]
        return acc

    wait_pages(cik_hbm, ikp, sems.at[0])

    def score_chunk(c, gmax):
        s = _dot(iq, chunk_of(ikp, c, IDX_DIM))
        gmax = list(gmax)
        for jj in range(ppc):
            t = jnp.maximum(s[:, jj * LANES:(jj + 1) * LANES], 0.0) * wb
            kj = head_sum(t)
            score_scr[:, pl.ds(pl.multiple_of(c * ck, ck) + jj * LANES, LANES)] = kj
            gmax[jj % 2] = jnp.maximum(gmax[jj % 2], kj)
        return tuple(gmax)
    floor = jnp.full((t_new, LANES), -jnp.inf, F32)
    g_even, g_odd = lax.fori_loop(0, n_ck, score_chunk, (floor, floor))

    pad_rows = lambda a: jnp.concatenate([a, jnp.zeros((LANES - t_new, a.shape[1]), a.dtype)], axis=0)
    new_ok = (lane <= qrow) & (lane < t_new)
    s_new = jnp.maximum(_dot_t(iq, pad_rows(iks[:, :IDX_DIM].astype(BF16))), 0.0) * wb
    k_new = jnp.where(new_ok, head_sum(s_new), -jnp.inf)
    score_scr[:, past:past + LANES] = k_new

    n_tiles = past // LANES + 1
    tile_idx = lambda jt: jt * LANES + lane

    def count_over(pred):
        def run(cand_b):
            cnt = [jnp.zeros((t_new, LANES), I32)] * 2
            for jt in range(n_tiles):
                kk = score_scr[:, jt * LANES:(jt + 1) * LANES]
                cnt[jt % 2] = cnt[jt % 2] + jnp.where(pred(kk, tile_idx(jt), cand_b), 1, 0)
            return _lane_sum(cnt[0] + cnt[1])
        return run

    def max_below(x_b):
        m = floor
        for jt in range(n_tiles):
            kk = score_scr[:, jt * LANES:(jt + 1) * LANES]
            m = jnp.maximum(m, jnp.where(kk < x_b, kk, -jnp.inf))
        return _lane_max(m)

    lo0, hi0 = _threshold_bracket(jnp.maximum(g_even, k_new), g_odd)
    done0 = jnp.zeros((t_new, 1), jnp.bool_)
    thr, tie, n_tie = _select_threshold(count_over(lambda kk, kidx, cb: kk >= cb), max_below, lo0, hi0, done0, topk)
    thr_b = _lane_bcast(thr)
    cut_scr[...] = jnp.full((t_new, LANES), INT_BIG, I32)

    @pl.when(jnp.max(tie.astype(I32)) > 0)
    def _():
        cut = _tie_cut(count_over(lambda kk, kidx, cb: (kk == thr_b) & (kidx < cb)), tie, n_tie, t_new, nbits)
        cut_scr[...] = _lane_bcast(cut)
    cut_b = cut_scr[...]

    def sel_rows(kk, kidx):
        return jnp.concatenate([_selected(kk, kidx, thr_b, cut_b)] * N_HEADS, axis=0)

    def sel_chunk(c, jj):
        kk = score_scr[:, pl.ds(pl.multiple_of(c * ck, ck) + jj * LANES, LANES)]
        return sel_rows(kk, c * ck + jj * LANES + lane)

    wait_pages(ck_hbm, kp, sems.at[1])
    wait_pages(cv_hbm, vp, sems.at[2])
    q = q_ref[0].reshape(rows, KV_W)
    s_new = _dot_t(q, pad_rows(kbn_ref[...]))
    sel_new = sel_rows(k_new, tile_idx(past // LANES))

    def max_chunk(c, m):
        s = _dot(q, chunk_of(kp, c, KV_W))
        for jj in range(ppc):
            m = jnp.maximum(m, jnp.where(sel_chunk(c, jj), s[:, jj * LANES:(jj + 1) * LANES], -jnp.inf))
        return m
    m = lax.fori_loop(0, n_ck, max_chunk, jnp.where(sel_new, s_new, -jnp.inf))
    m_b = _lane_bcast(_lane_max(m))

    p_new = jnp.where(sel_new, jnp.exp2(s_new - m_b), 0.0)
    acc0 = _dot(p_new.astype(BF16), pad_rows(vbn_ref[...]))

    def pv_chunk(c, carry):
        l, acc = carry
        s = _dot(q, chunk_of(kp, c, KV_W))
        ps = []
        for jj in range(ppc):
            p = jnp.where(sel_chunk(c, jj), jnp.exp2(s[:, jj * LANES:(jj + 1) * LANES] - m_b), 0.0)
            l = l + p
            ps.append(p.astype(BF16))
        return l, acc + _dot_t(jnp.concatenate(ps, axis=1), chunk_of(vp, c, KV_W))
    l, acc = lax.fori_loop(0, n_ck, pv_chunk, (p_new, acc0))
    o = acc / _lane_sum(l)
    o_ref[0] = o.reshape(N_HEADS, t_new, KV_W).astype(BF16)


def _attn_sample(page_table, iq_s, q_s, iks_s, kb_s, vb_s, cache_ikt, cache_kt, cache_vt, topk):
    db, n_pages = page_table.shape
    t_new = iq_s.shape[2]
    past = n_pages * PAGE_SIZE
    assert past >= 2 * LANES and past + 1 > topk, "the threshold bracket needs two full key tiles"
    ppc = 4 if n_pages % 4 == 0 else 2
    nbits = max(1, int(np.ceil(np.log2(past + LANES))))
    grid_spec = pltpu.PrefetchScalarGridSpec(
        num_scalar_prefetch=1,
        grid=(db,),
        in_specs=[pl.BlockSpec((1, IDX_HEADS, t_new, IDX_DIM), lambda b, pt: (b, 0, 0, 0)),
                  pl.BlockSpec((1, N_HEADS, t_new, KV_W), lambda b, pt: (b, 0, 0, 0)),
                  pl.BlockSpec((t_new, LANES), lambda b, pt: (b, 0)),
                  pl.BlockSpec((t_new, KV_W), lambda b, pt: (b, 0)),
                  pl.BlockSpec((t_new, KV_W), lambda b, pt: (b, 0)),
                  pl.BlockSpec(memory_space=pl.ANY), pl.BlockSpec(memory_space=pl.ANY),
                  pl.BlockSpec(memory_space=pl.ANY)],
        out_specs=pl.BlockSpec((1, N_HEADS, t_new, KV_W), lambda b, pt: (b, 0, 0, 0)),
        scratch_shapes=[pltpu.VMEM((n_pages, IDX_DIM, PAGE_SIZE), F32),
                        pltpu.VMEM((n_pages, N_KV_HEADS, HEAD_DIM, PAGE_SIZE), F32),
                        pltpu.VMEM((n_pages, N_KV_HEADS, HEAD_DIM, PAGE_SIZE), F32),
                        pltpu.SemaphoreType.DMA((3,)),
                        pltpu.VMEM((t_new, past + LANES), F32),
                        pltpu.VMEM((t_new, LANES), I32)])
    return pl.pallas_call(
        functools.partial(_attn_sample_kernel, n_pages=n_pages, t_new=t_new, ppc=ppc, topk=topk, nbits=nbits),
        grid_spec=grid_spec,
        out_shape=jax.ShapeDtypeStruct((db, N_HEADS, t_new, KV_W), BF16),
        compiler_params=_params("arbitrary"),
        name="attn_sample",
    )(page_table, iq_s, q_s, iks_s, kb_s, vb_s, cache_ikt, cache_kt, cache_vt)


def _mix_kernel(attn_ref, sga_ref, yc_ref, x_ref, g1_ref, sh2_ref, sc2_ref, n2_ref, woa_ref, wout_ref,
                wr_hi_ref, wr_lo_ref, br_ref, x1_ref, h2_ref, comb_ref, *, sample, tm):
    if sample:
        a = attn_ref[...]
        y_attn = jnp.zeros((tm, D_MODEL), F32)
        for h in range(N_HEADS):
            y_attn = y_attn + _dot(a[:, h].reshape(tm, KV_W), woa_ref[h])
    else:
        y_attn = _dot(attn_ref[...], woa_ref[...])
    mixed = _dot((sga_ref[...] * y_attn + yc_ref[...]).astype(BF16), wout_ref[...])
    x3 = x_ref[...]
    g, r, _ = x3.shape
    x1 = x3 + g1_ref[...] * mixed.reshape(g, r, D_MODEL)
    x1_ref[...] = x1
    h2 = _modnorm(x1, n2_ref[...], sc2_ref[...], sh2_ref[...])
    h2_ref[...] = h2.astype(BF16)

    lg = _dot3(h2, wr_hi_ref[...], wr_lo_ref[...]) + br_ref[...]
    lane = lax.broadcasted_iota(I32, (tm, LANES), 1)
    is_g = lane < N_GROUPS
    is_e = (lane >= ROUTER_E0) & (lane < ROUTER_E0 + N_EXPERTS)
    first = lambda msk: _lane_min(jnp.where(msk, lane, LANES))
    gl = jnp.where(is_g, lg, -jnp.inf)
    gmax = _lane_max(gl)
    g_sel = first(gl == gmax)
    g_w = 1.0 / _lane_sum(jnp.where(is_g, jnp.exp(lg - gmax), 0.0))
    in_grp = is_e & (((lane - ROUTER_E0) >> 2) == g_sel)
    me = jnp.where(in_grp, lg, -jnp.inf)
    v1 = _lane_max(me)
    i1 = first(in_grp & (me == v1))
    rest = in_grp & (lane != i1)
    me2 = jnp.where(rest, lg, -jnp.inf)
    v2 = _lane_max(me2)
    i2 = first(rest & (me2 == v2))
    e2 = jnp.exp(v2 - v1)
    den = 1.0 + e2
    comb_ref[...] = jnp.where(lane == i1, g_w / den, 0.0) + jnp.where(lane == i2, g_w * e2 / den, 0.0)


def _mix(attn, sga, yc, x3, mod3, mod_row0, norm2_g, w_oa, w_out, wr_hi, wr_lo, b_r, *, sample):
    G, R, _ = x3.shape
    n = G * R
    if sample:
        bb = min(G, 32)
        tm = bb * R
        nb = G // bb
        xspec = pl.BlockSpec((bb, R, D_MODEL), lambda i: (i, 0, 0))
        mspec = lambda c: pl.BlockSpec((bb, 1, D_MODEL), lambda i: (i, 0, c))
        aspec = pl.BlockSpec((bb, N_HEADS, R, KV_W), lambda i: (i, 0, 0, 0))
        woa_spec = _resident((N_HEADS, KV_W, D_MODEL))
    else:
        tm = min(R, 256)
        nb = R // tm
        xspec = pl.BlockSpec((1, tm, D_MODEL), lambda i: (0, i, 0))
        mspec = lambda c: pl.BlockSpec((1, 1, D_MODEL), lambda i: (mod_row0, 0, c))
        aspec = pl.BlockSpec((tm, ATTN_W), lambda i: (i, 0))
        woa_spec = _resident((ATTN_W, D_MODEL))
    row = lambda w: pl.BlockSpec((tm, w), lambda i: (i, 0))
    return pl.pallas_call(
        functools.partial(_mix_kernel, sample=sample, tm=tm),
        grid=(nb,),
        in_specs=[aspec, row(D_MODEL), row(D_MODEL), xspec, mspec(2), mspec(3), mspec(4), _resident((1, D_MODEL)),
                  woa_spec, _resident((D_MODEL, D_MODEL)), _resident((D_MODEL, LANES)), _resident((D_MODEL, LANES)),
                  _resident((1, LANES))],
        out_specs=[xspec, row(D_MODEL), row(LANES)],
        out_shape=[jax.ShapeDtypeStruct((G, R, D_MODEL), F32), jax.ShapeDtypeStruct((n, D_MODEL), BF16),
                   jax.ShapeDtypeStruct((n, LANES), F32)],
        compiler_params=_params("arbitrary"),
        name="mix_sample" if sample else "mix_prompt",
    )(attn, sga, yc, x3, mod3, mod3, mod3, norm2_g.reshape(1, D_MODEL), w_oa, w_out, wr_hi, wr_lo, b_r)


def _moe_kernel(h2_ref, comb_ref, x1_ref, g2_ref, fg_ref, weg_ref, weu_ref, wed_ref, y_ref, acc_scr, *, tm, epb):
    j = pl.program_id(1)

    @pl.when(j == 0)
    def _():
        acc_scr[...] = jnp.zeros(acc_scr.shape, F32)

    h2 = h2_ref[...]
    a = _dot(h2, weg_ref[...])
    u = _dot(h2, weu_ref[...])
    hid = a * _sigmoid(a) * u
    comb = comb_ref[...]
    lane = lax.broadcasted_iota(I32, (tm, LANES), 1)
    parts = []
    for e in range(epb):
        ce = _lane_sum(jnp.where(lane == ROUTER_E0 + j * epb + e, comb, 0.0))
        parts.append((hid[:, e * EXPERT_FF:(e + 1) * EXPERT_FF] * ce).astype(BF16))
    acc_scr[...] += _dot(jnp.concatenate(parts, axis=1), wed_ref[...])

    @pl.when(j == pl.num_programs(1) - 1)
    def _():
        x1 = x1_ref[...]
        g, r, _ = x1.shape
        x2 = x1 + g2_ref[...] * acc_scr[...].reshape(g, r, D_MODEL)
        ms = jnp.mean(x2 * x2, axis=-1, keepdims=True)
        y_ref[...] = x2 * lax.rsqrt(ms + EPS) * fg_ref[...][None]


def _moe(h2, comb, x1, mod3, mod_row0, final_g, w_eg, w_eu, w_ed, *, sample):
    G, R, _ = x1.shape
    epb = EXP_PER_GROUP
    nj = N_EXPERTS // epb
    wblk = epb * EXPERT_FF
    if sample:
        bb = min(G, 64)
        tm = bb * R
        nb = G // bb
        xspec = pl.BlockSpec((bb, R, D_MODEL), lambda i, j: (i, 0, 0))
        g2spec = pl.BlockSpec((bb, 1, D_MODEL), lambda i, j: (i, 0, 5))
    else:
        tm = min(R, 512)
        nb = R // tm
        xspec = pl.BlockSpec((1, tm, D_MODEL), lambda i, j: (0, i, 0))
        g2spec = pl.BlockSpec((1, 1, D_MODEL), lambda i, j: (mod_row0, 0, 5))
    return pl.pallas_call(
        functools.partial(_moe_kernel, tm=tm, epb=epb),
        grid=(nb, nj),
        in_specs=[pl.BlockSpec((tm, D_MODEL), lambda i, j: (i, 0)),
                  pl.BlockSpec((tm, LANES), lambda i, j: (i, 0)),
                  xspec, g2spec,
                  pl.BlockSpec((1, D_MODEL), lambda i, j: (0, 0)),
                  pl.BlockSpec((D_MODEL, wblk), lambda i, j: (0, j)),
                  pl.BlockSpec((D_MODEL, wblk), lambda i, j: (0, j)),
                  pl.BlockSpec((wblk, D_MODEL), lambda i, j: (j, 0))],
        out_specs=xspec,
        out_shape=jax.ShapeDtypeStruct((G, R, D_MODEL), F32),
        scratch_shapes=[pltpu.VMEM((tm, D_MODEL), F32)],
        compiler_params=_params("arbitrary", "arbitrary"),
        name="moe_sample" if sample else "moe_prompt",
    )(h2, comb, x1, mod3, final_g.reshape(1, D_MODEL), w_eg, w_eu, w_ed)


def kernel(x_prompt, x_sample, cache_k, cache_v, cache_idx_k, state_conv, page_table, c_prompt, c_sample,
           w_ada, b_ada, norm1_g, norm2_g, w_in_mix, conv_w, w_o_attn, w_o_conv, w_out,
           w_router_group, b_router_group, w_router_expert, b_router_expert,
           w_exp_gate, w_exp_up, w_exp_down, final_norm_g):
    depth = w_ada.shape[0]
    assert depth == 1, "single-layer step"
    B, S, _ = x_prompt.shape
    assert B == 1, "one prompt sequence"
    DB, T, _ = x_sample.shape
    assert T == SUBLANES, "sample tokens per request must fill one sublane tile"
    n_pages = page_table.shape[1]
    past = n_pages * PAGE_SIZE

    n_rows = DB + B
    pad = (-n_rows) % SUBLANES
    c_all = jnp.concatenate([c_sample, c_prompt, jnp.zeros((pad, D_MODEL), F32)], axis=0)
    mod = _ada(c_all, w_ada[0], b_ada[0])
    mod3 = mod.reshape(n_rows + pad, 1, 6 * D_MODEL)

    half = HEAD_DIM // 8
    inv_freq = ROPE_THETA ** (-jnp.arange(half, dtype=F32) / half)
    lane = np.arange(LANES)
    invf = jnp.where((lane % HEAD_DIM) < 2 * half, inv_freq[lane % half], 0.0).reshape(1, LANES).astype(F32)
    w_in = w_in_mix[0]
    w_oc = w_o_conv[0].astype(BF16)
    w_oa = w_o_attn[0]
    w_oa_p = w_oa.astype(BF16)
    w_oa_s = jnp.stack([
        jnp.concatenate([jnp.zeros((HEAD_DIM * (h // REP), D_MODEL), F32), w_oa[HEAD_DIM * h:HEAD_DIM * (h + 1)],
                         jnp.zeros((KV_W - HEAD_DIM * (h // REP + 1), D_MODEL), F32)], axis=0)
        for h in range(N_HEADS)]).astype(BF16)
    w_o = w_out[0].astype(BF16)
    zr = jnp.zeros((D_MODEL, LANES - ROUTER_E0 - N_EXPERTS), F32)
    w_r = jnp.concatenate([w_router_group[0], w_router_expert[0], zr], axis=1)
    wr_hi = w_r.astype(BF16)
    wr_lo = (w_r - wr_hi.astype(F32)).astype(BF16)
    b_r = jnp.concatenate([b_router_group[0], b_router_expert[0],
                           jnp.zeros((LANES - ROUTER_E0 - N_EXPERTS,), F32)]).reshape(1, LANES)
    cat_e = lambda w: jnp.transpose(w, (1, 0, 2)).reshape(D_MODEL, N_EXPERTS * EXPERT_FF).astype(BF16)
    w_eg = cat_e(w_exp_gate[0])
    w_eu = cat_e(w_exp_up[0])
    w_ed = w_exp_down[0].reshape(N_EXPERTS * EXPERT_FF, D_MODEL).astype(BF16)

    xp3 = x_prompt
    (q8, iq8, iks_p, k_p, v_p, kt, ikt, va_p, yc_p, sga_p, cst_p) = _proj(
        xp3, mod3, DB, norm1_g[0], _proj_weight(w_in, False), invf, conv_w[0], w_oc, None, sample=False, pos_base=0)
    attn_p = _attn_prompt(iq8, q8, iks_p, ikt, kt, va_p, min(TOPK_MAX, S // 4))
    x1_p, h2_p, comb_p = _mix(attn_p, sga_p, yc_p, xp3, mod3, DB, norm2_g[0], w_oa_p, w_o, wr_hi, wr_lo, b_r,
                              sample=False)
    y_p = _moe(h2_p, comb_p, x1_p, mod3, DB, final_norm_g, w_eg, w_eu, w_ed, sample=False)

    cache_ikt = jnp.transpose(cache_idx_k[0], (0, 2, 1))
    cache_kt = jnp.transpose(cache_k[0], (0, 2, 3, 1))
    cache_vt = jnp.transpose(cache_v[0], (0, 2, 3, 1))
    (q_s, iq_s, iks_s, k_s, v_s, kb_s, vb_s, yc_s, sga_s, cst_s) = _proj(
        x_sample, mod3, 0, norm1_g[0], _proj_weight(w_in, True), invf, conv_w[0], w_oc, state_conv[0],
        sample=True, pos_base=past)
    attn_s = _attn_sample(page_table, iq_s, q_s, iks_s, kb_s, vb_s, cache_ikt, cache_kt, cache_vt,
                          min(TOPK_MAX, (past + T) // 4))
    x1_s, h2_s, comb_s = _mix(attn_s, sga_s, yc_s, x_sample, mod3, 0, norm2_g[0], w_oa_s, w_o, wr_hi, wr_lo, b_r,
                              sample=True)
    y_s = _moe(h2_s, comb_s, x1_s, mod3, 0, final_norm_g, w_eg, w_eu, w_ed, sample=True)

    return (y_p, y_s,
            k_p.reshape(1, B, S, N_KV_HEADS, HEAD_DIM), v_p.reshape(1, B, S, N_KV_HEADS, HEAD_DIM),
            iks_p[:, :IDX_DIM].reshape(1, B, S, IDX_DIM),
            cst_p[SUBLANES - (CONV_K - 1):].reshape(1, B, CONV_K - 1, CONV_W),
            k_s.reshape(1, DB, T, N_KV_HEADS, HEAD_DIM), v_s.reshape(1, DB, T, N_KV_HEADS, HEAD_DIM),
            iks_s[:, :IDX_DIM].reshape(1, DB, T, IDX_DIM),
            cst_s[:, T - (CONV_K - 1):].reshape(1, DB, CONV_K - 1, CONV_W))
```

```python
import functools

import jax
import jax.numpy as jnp
import numpy as np
from jax import lax
from jax.experimental import pallas as pl
from jax.experimental.pallas import tpu as pltpu

F32 = jnp.float32
BF16 = jnp.bfloat16
I32 = jnp.int32

D_MODEL = 1024
N_HEADS = 8
N_KV_HEADS = 4
HEAD_DIM = 64
ATTN_W = N_HEADS * HEAD_DIM
KV_W = N_KV_HEADS * HEAD_DIM
ROPE_THETA = 500000.0
IDX_HEADS = 8
IDX_DIM = 64
TOPK_MAX = 256
CONV_W = 512
CONV_K = 3
N_GROUPS = 4
EXP_PER_GROUP = 4
N_EXPERTS = N_GROUPS * EXP_PER_GROUP
EXPERT_FF = 256
EPS = 1e-6
PAGE_SIZE = 128

LANES = 128
SUBLANES = 8
VMEM_LIMIT = 56 * 1024 * 1024
INT_BIG = 2 ** 30
BIG_SCORE = 3.0e38
CHUNKS_PER_STEP = 4
BISECT_STEPS = 13
LOG2E = 1.4426950408889634
Q_SCALE = HEAD_DIM ** -0.5 * LOG2E
IW_LANE = 112
ROUTER_E0 = 4
REP = N_HEADS // N_KV_HEADS


def _dot(a, b):
    return jnp.dot(a, b, preferred_element_type=F32)


def _dot_t(a, b):
    return lax.dot_general(a, b, (((1,), (1,)), ((), ())), preferred_element_type=F32)


def _split_bf16(a):
    hi = a.astype(BF16)
    lo = (a - hi.astype(F32)).astype(BF16)
    return hi, lo


def _dot3(a, w_hi, w_lo):
    a_hi, a_lo = _split_bf16(a)
    return _dot(a_hi, w_hi) + (_dot(a_hi, w_lo) + _dot(a_lo, w_hi))


def _sigmoid(x):
    return 1.0 / (1.0 + jnp.exp(-x))


def _params(*sem):
    return pltpu.CompilerParams(dimension_semantics=sem, vmem_limit_bytes=VMEM_LIMIT)


def _resident(shape):
    nd = len(shape)
    return pl.BlockSpec(shape, lambda *_: (0,) * nd, pipeline_mode=pl.Buffered(1))


def _ada_kernel(c_ref, w_ref, b_ref, o_ref):
    c = c_ref[...]
    a = c * _sigmoid(c)
    w = w_ref[...]
    w_hi, w_lo = _split_bf16(w)
    o_ref[...] = _dot3(a, w_hi, w_lo) + b_ref[...]


def _ada(c_all, w_ada, b_ada):
    rows = c_all.shape[0]
    n = w_ada.shape[1]
    tn = 512
    return pl.pallas_call(
        _ada_kernel,
        grid=(n // tn,),
        in_specs=[pl.BlockSpec((rows, D_MODEL), lambda j: (0, 0)),
                  pl.BlockSpec((D_MODEL, tn), lambda j: (0, j)),
                  pl.BlockSpec((1, tn), lambda j: (0, j))],
        out_specs=pl.BlockSpec((rows, tn), lambda j: (0, j)),
        out_shape=jax.ShapeDtypeStruct((rows, n), F32),
        compiler_params=_params("arbitrary"),
        name="ada",
    )(c_all, w_ada, b_ada.reshape(1, n))


def _modnorm(x3, g, sc3, sh3):
    ms = jnp.mean(x3 * x3, axis=-1, keepdims=True)
    xn = x3 * lax.rsqrt(ms + EPS)
    h = xn * g[None] * (1.0 + sc3) + sh3
    return h.reshape(x3.shape[0] * x3.shape[1], x3.shape[2])


def _proj_kernel(*refs, sample, tm, pos_base, cols):
    if sample:
        (x_ref, sh_ref, sc_ref, g_ref, w_ref, invf_ref, cw_ref, woc_ref, st_ref,
         qo_ref, iqo_ref, iks_ref, k_ref, v_ref, kb_ref, vb_ref, yc_ref, sga_ref, cst_ref) = refs
    else:
        (x_ref, sh_ref, sc_ref, g_ref, w_ref, invf_ref, cw_ref, woc_ref,
         qo_ref, iqo_ref, iks_ref, k_ref, v_ref, kt_ref, ikt_ref, vb_ref, yc_ref, sga_ref, cst_ref,
         cu_scr) = refs
    i = pl.program_id(0)
    hb = _modnorm(x_ref[...], g_ref[...], sc_ref[...], sh_ref[...]).astype(BF16)

    def proj(name):
        c0, c1 = cols[name]
        return _dot(hb, w_ref[:, c0:c1])

    lane = lax.broadcasted_iota(I32, (tm, LANES), 1)
    row = lax.broadcasted_iota(I32, (tm, LANES), 0)
    if sample:
        pos = pos_base + (row & (SUBLANES - 1))
    else:
        pos = i * tm + row
    ang = pos.astype(F32) * invf_ref[...]
    cosv = jnp.cos(ang)
    sinv = jnp.sin(ang)
    j = lane & (HEAD_DIM - 1)
    half = HEAD_DIM // 8
    rc = jnp.where(j < 2 * half, cosv, 1.0)
    rs_lo = jnp.where((j >= half) & (j < 2 * half), sinv, 0.0)
    rs_hi = jnp.where(j < half, -sinv, 0.0)

    def rope(xs):
        return xs * rc + pltpu.roll(xs, half, 1) * rs_lo + pltpu.roll(xs, LANES - half, 1) * rs_hi

    pq = proj("q")
    if sample:
        for h in range(N_HEADS):
            blk = pq[:, KV_W * h:KV_W * (h + 1)]
            r = jnp.concatenate([rope(blk[:, :LANES]), rope(blk[:, LANES:])], axis=1) * Q_SCALE
            qo_ref[:, h] = r.reshape(tm // SUBLANES, SUBLANES, KV_W).astype(BF16)
    else:
        for s in range(ATTN_W // LANES):
            r = (rope(pq[:, LANES * s:LANES * (s + 1)]) * Q_SCALE).astype(BF16)
            qo_ref[2 * s] = r[:, :HEAD_DIM]
            qo_ref[2 * s + 1] = r[:, HEAD_DIM:]

    pk = proj("k")
    for s in range(KV_W // LANES):
        rk = rope(pk[:, LANES * s:LANES * (s + 1)])
        k_ref[:, LANES * s:LANES * (s + 1)] = rk
        if sample:
            kb_ref[:, LANES * s:LANES * (s + 1)] = rk.astype(BF16)
        else:
            kt_ref[LANES * s:LANES * (s + 1), :] = rk.T.astype(BF16)

    pv = proj("v")
    v_ref[...] = pv
    if sample:
        vb_ref[...] = pv.astype(BF16)
    else:
        tail = jnp.where(lane == HEAD_DIM, 1.0, 0.0)
        for s in range(KV_W // LANES):
            slab = pv[:, LANES * s:LANES * (s + 1)]
            vb_ref[:, LANES * 2 * s:LANES * (2 * s + 1)] = jnp.where(lane < HEAD_DIM, slab, tail).astype(BF16)
            vb_ref[:, LANES * (2 * s + 1):LANES * (2 * s + 2)] = jnp.where(
                lane < HEAD_DIM, pltpu.roll(slab, HEAD_DIM, 1), tail).astype(BF16)

    piq = proj("iq")
    for s in range(IDX_HEADS * IDX_DIM // LANES):
        r = (rope(piq[:, LANES * s:LANES * (s + 1)]) * (IDX_DIM ** -0.5)).astype(BF16)
        if sample:
            iqo_ref[:, 2 * s] = r[:, :IDX_DIM].reshape(tm // SUBLANES, SUBLANES, IDX_DIM)
            iqo_ref[:, 2 * s + 1] = r[:, IDX_DIM:].reshape(tm // SUBLANES, SUBLANES, IDX_DIM)
        else:
            iqo_ref[2 * s] = r[:, :IDX_DIM]
            iqo_ref[2 * s + 1] = r[:, IDX_DIM:]

    slab = rope(proj("iks"))
    iks_ref[...] = slab
    if not sample:
        ikt_ref[...] = slab.T[:IDX_DIM, :].astype(BF16)

    cu = proj("gc") * proj("u")
    cw = cw_ref[...]
    if sample:
        g8 = tm // SUBLANES
        st = st_ref[...]
        grp = (g8, SUBLANES, CONV_W)
        s0 = jnp.broadcast_to(st[:, 0:1, :], grp).reshape(tm, CONV_W)
        s1 = jnp.broadcast_to(st[:, 1:2, :], grp).reshape(tm, CONV_W)
        t = lax.broadcasted_iota(I32, (tm, CONV_W), 0) & (SUBLANES - 1)
        m1 = jnp.where(t >= 1, pltpu.roll(cu, 1, 0), s1)
        m2 = jnp.where(t >= 2, pltpu.roll(cu, 2, 0), jnp.where(t == 1, s1, s0))
        conv = m2 * cw[0:1] + m1 * cw[1:2] + cu * cw[2:3]
        cst_ref[...] = cu.reshape(grp)
    else:
        @pl.when(i == 0)
        def _():
            cu_scr[0:SUBLANES, :] = jnp.zeros((SUBLANES, CONV_W), F32)
        cu_scr[SUBLANES:SUBLANES + tm, :] = cu
        m1 = cu_scr[SUBLANES - 1:SUBLANES - 1 + tm, :]
        m2 = cu_scr[SUBLANES - 2:SUBLANES - 2 + tm, :]
        conv = m2 * cw[0:1] + m1 * cw[1:2] + cu * cw[2:3]
        tail = cu_scr[tm:tm + SUBLANES, :]
        cu_scr[0:SUBLANES, :] = tail
        cst_ref[...] = tail
    zc = (proj("gb_conv") * conv).astype(BF16)
    yconv = _dot(zc, woc_ref[...])
    sga_ref[...] = _sigmoid(proj("ga"))
    yc_ref[...] = _sigmoid(proj("gb")) * yconv


def _proj_layout(sample):
    names = [("q", N_HEADS * KV_W if sample else ATTN_W), ("k", KV_W), ("v", KV_W),
             ("iq", IDX_HEADS * IDX_DIM), ("iks", LANES), ("u", CONV_W), ("gb_conv", CONV_W), ("gc", CONV_W),
             ("ga", D_MODEL), ("gb", D_MODEL)]
    cols, c = {}, 0
    for n, w in names:
        cols[n] = (c, c + w)
        c += w
    return cols, c


def _proj_weight(w_in, sample):
    splits = (ATTN_W, KV_W, KV_W, IDX_HEADS * IDX_DIM, IDX_DIM, IDX_HEADS, CONV_W, CONV_W, CONV_W, D_MODEL, D_MODEL)
    parts, c = [], 0
    for w in splits:
        parts.append(w_in[:, c:c + w])
        c += w
    wq, wk, wv, wiq, wik, wiw, wu, wgb, wgc, wga, wgbr = parts
    z = lambda n: jnp.zeros((D_MODEL, n), w_in.dtype)
    if sample:
        qparts = []
        for h in range(N_HEADS):
            g = h // REP
            qparts += [z(HEAD_DIM * g), wq[:, HEAD_DIM * h:HEAD_DIM * (h + 1)], z(KV_W - HEAD_DIM * (g + 1))]
        wq = jnp.concatenate(qparts, axis=1)
    slab = jnp.concatenate([wik, z(IW_LANE - IDX_DIM), wiw, z(LANES - IW_LANE - IDX_HEADS)], axis=1)
    return jnp.concatenate([wq, wk, wv, wiq, slab, wu, wgb, wgc, wga, wgbr], axis=1).astype(BF16)


def _proj(x3, mod3, mod_row0, norm_g, w_p, invf, conv_w, w_oc, state, *, sample, pos_base):
    G, R, _ = x3.shape
    n = G * R
    cols, wtot = _proj_layout(sample)
    if sample:
        bb = min(G, 64)
        tm = bb * R
        nb = G // bb
        xspec = pl.BlockSpec((bb, R, D_MODEL), lambda i: (i, 0, 0))
        shspec = pl.BlockSpec((bb, 1, D_MODEL), lambda i: (i, 0, 0))
        scspec = pl.BlockSpec((bb, 1, D_MODEL), lambda i: (i, 0, 1))
    else:
        tm = min(R, 512)
        nb = R // tm
        xspec = pl.BlockSpec((1, tm, D_MODEL), lambda i: (0, i, 0))
        shspec = pl.BlockSpec((1, 1, D_MODEL), lambda i: (mod_row0, 0, 0))
        scspec = pl.BlockSpec((1, 1, D_MODEL), lambda i: (mod_row0, 0, 1))
    row = lambda w: pl.BlockSpec((tm, w), lambda i: (i, 0))
    in_specs = [xspec, shspec, scspec, _resident((1, D_MODEL)), _resident((D_MODEL, wtot)), _resident((1, LANES)),
                _resident((CONV_K, CONV_W)), _resident((CONV_W, D_MODEL))]
    args = [x3, mod3, mod3, norm_g.reshape(1, D_MODEL), w_p, invf, conv_w, w_oc]
    if sample:
        in_specs.append(pl.BlockSpec((bb, CONV_K - 1, CONV_W), lambda i: (i, 0, 0)))
        args.append(state)
        out_specs = [pl.BlockSpec((bb, N_HEADS, R, KV_W), lambda i: (i, 0, 0, 0)),
                     pl.BlockSpec((bb, IDX_HEADS, R, IDX_DIM), lambda i: (i, 0, 0, 0)),
                     row(LANES), row(KV_W), row(KV_W), row(KV_W), row(KV_W), row(D_MODEL), row(D_MODEL),
                     pl.BlockSpec((bb, R, CONV_W), lambda i: (i, 0, 0))]
        out_shape = [jax.ShapeDtypeStruct((G, N_HEADS, R, KV_W), BF16),
                     jax.ShapeDtypeStruct((G, IDX_HEADS, R, IDX_DIM), BF16),
                     jax.ShapeDtypeStruct((n, LANES), F32),
                     jax.ShapeDtypeStruct((n, KV_W), F32), jax.ShapeDtypeStruct((n, KV_W), F32),
                     jax.ShapeDtypeStruct((n, KV_W), BF16), jax.ShapeDtypeStruct((n, KV_W), BF16),
                     jax.ShapeDtypeStruct((n, D_MODEL), F32), jax.ShapeDtypeStruct((n, D_MODEL), F32),
                     jax.ShapeDtypeStruct((G, R, CONV_W), F32)]
        scratch = []
    else:
        out_specs = [pl.BlockSpec((N_HEADS, tm, HEAD_DIM), lambda i: (0, i, 0)),
                     pl.BlockSpec((IDX_HEADS, tm, IDX_DIM), lambda i: (0, i, 0)),
                     row(LANES), row(KV_W), row(KV_W),
                     pl.BlockSpec((KV_W, tm), lambda i: (0, i)), pl.BlockSpec((IDX_DIM, tm), lambda i: (0, i)),
                     row(N_KV_HEADS * LANES), row(D_MODEL), row(D_MODEL),
                     pl.BlockSpec((SUBLANES, CONV_W), lambda i: (0, 0))]
        out_shape = [jax.ShapeDtypeStruct((N_HEADS, n, HEAD_DIM), BF16),
                     jax.ShapeDtypeStruct((IDX_HEADS, n, IDX_DIM), BF16),
                     jax.ShapeDtypeStruct((n, LANES), F32),
                     jax.ShapeDtypeStruct((n, KV_W), F32), jax.ShapeDtypeStruct((n, KV_W), F32),
                     jax.ShapeDtypeStruct((KV_W, n), BF16), jax.ShapeDtypeStruct((IDX_DIM, n), BF16),
                     jax.ShapeDtypeStruct((n, N_KV_HEADS * LANES), BF16),
                     jax.ShapeDtypeStruct((n, D_MODEL), F32), jax.ShapeDtypeStruct((n, D_MODEL), F32),
                     jax.ShapeDtypeStruct((SUBLANES, CONV_W), F32)]
        scratch = [pltpu.VMEM((tm + SUBLANES, CONV_W), F32)]
    return pl.pallas_call(
        functools.partial(_proj_kernel, sample=sample, tm=tm, pos_base=pos_base, cols=cols),
        grid=(nb,), in_specs=in_specs, out_specs=out_specs, out_shape=out_shape, scratch_shapes=scratch,
        compiler_params=_params("arbitrary"),
        name="proj_sample" if sample else "proj_prompt",
    )(*args)


def _lane_bcast(col):
    return jnp.broadcast_to(col, (col.shape[0], LANES))


def _lane_min(a):
    return jnp.min(a, axis=1, keepdims=True)


def _lane_max(a):
    return jnp.max(a, axis=1, keepdims=True)


def _lane_sum(a):
    return jnp.sum(a, axis=1, keepdims=True)


def _threshold_bracket(g_even, g_odd):
    lo = jnp.maximum(jnp.minimum(_lane_min(g_even), _lane_min(g_odd)), -BIG_SCORE)
    top = jnp.maximum(_lane_max(g_even), _lane_max(g_odd))
    return lo, top + jnp.maximum(jnp.abs(top) * 2.0 ** -10, 1e-30)


def _select_threshold(count_ge, max_below, lo, hi, done0, topk):
    live0 = jnp.logical_not(done0)

    def bisect(_, st):
        lo, hi, c_lo, c_hi = st
        act = live0 & (c_lo != topk)
        mid = 0.5 * lo + 0.5 * hi
        c = count_ge(mid)
        up = act & (c >= topk)
        dn = act & (c < topk)
        return jnp.where(up, mid, lo), jnp.where(dn, mid, hi), jnp.where(up, c, c_lo), jnp.where(dn, c, c_hi)

    lo, hi, c_lo, c_hi = lax.fori_loop(
        0, BISECT_STEPS, bisect, (lo, hi, jnp.full(lo.shape, INT_BIG, I32), jnp.zeros(lo.shape, I32)))

    def snap_cond(st):
        return jnp.max(st[4]) > 0

    def snap(st):
        lo, hi, c_lo, c_hi, open_ = st
        cand = max_below(hi)
        c = count_ge(cand)
        found = (open_ > 0) & (c >= topk)
        down = (open_ > 0) & (c < topk)
        return (jnp.where(found, cand, lo), jnp.where(down, cand, hi), jnp.where(found, c, c_lo),
                jnp.where(down, c, c_hi), down.astype(I32))

    lo, hi, c_lo, c_hi, _ = lax.while_loop(
        snap_cond, snap, (lo, hi, c_lo, c_hi, (live0 & (c_lo != topk)).astype(I32)))
    tie = live0 & (c_lo != topk)
    return jnp.where(done0, -BIG_SCORE, lo), tie, topk - c_hi


def _tie_cut(count_tie_below, need, r, nbits):
    def bit_step(b, x):
        cand = x + lax.shift_left(jnp.int32(1), nbits - 1 - b)
        return jnp.where(count_tie_below(cand) < r, cand, x)
    x = lax.fori_loop(0, nbits, bit_step, jnp.zeros(r.shape, I32))
    return jnp.where(need, x, INT_BIG)


def _selected(kk, kidx, thr_b, cut_b):
    return (kk > thr_b) | ((kk == thr_b) & (kidx <= cut_b))


def _attn_prompt_kernel(iq_ref, q_ref, iks_ref, ikt_ref, kt_ref, va_ref, o_ref,
                        score_scr, wb_scr, cut_scr, m_scr, qa_scr, acc_scr, *, tq, ck, topk, nbits):
    qi = pl.program_id(0)
    pair = CHUNKS_PER_STEP * ck
    n_pair = (qi * tq + tq + pair - 1) // pair
    nsub = ck // LANES

    w = iks_ref[:, IW_LANE:IW_LANE + IDX_HEADS] * (IDX_HEADS ** -0.5)
    for h in range(IDX_HEADS):
        wb_scr[h] = _lane_bcast(w[:, h:h + 1])
    qpos = qi * tq + lax.broadcasted_iota(I32, (tq, LANES), 0)
    lane = lax.broadcasted_iota(I32, (tq, LANES), 1)

    def for_chunks(fn, init):
        def body(c2, carry):
            for part in range(CHUNKS_PER_STEP):
                carry = fn(pl.multiple_of(c2 * pair + part * ck, ck), carry)
            return carry
        return lax.fori_loop(0, n_pair, body, init)

    def tile_of(k0):
        return lax.shift_right_logical(k0, LANES.bit_length() - 1)

    def score_chunk(k0, gmax):
        ikc = ikt_ref[:, pl.ds(k0, ck)]
        acc = jnp.zeros((tq, ck), F32)
        for h in range(IDX_HEADS):
            s = _dot(iq_ref[h], ikc)
            wbh = wb_scr[h]
            acc = acc + jnp.maximum(s, 0.0) * jnp.concatenate([wbh] * nsub, axis=1)
        gmax = list(gmax)
        for jj in range(nsub):
            kidx = k0 + jj * LANES + lane
            kj = jnp.where(kidx <= qpos, acc[:, jj * LANES:(jj + 1) * LANES], -jnp.inf)
            score_scr[tile_of(k0) + jj] = kj
            gmax[jj % 2] = jnp.maximum(gmax[jj % 2], kj)
        return tuple(gmax)
    floor = jnp.full((tq, LANES), -jnp.inf, F32)
    g_even, g_odd = for_chunks(score_chunk, (floor, floor))

    def count_ge(x_b):
        def body(k0, cnt):
            cnt = list(cnt)
            for jj in range(nsub):
                cnt[jj % 2] = cnt[jj % 2] + jnp.where(score_scr[tile_of(k0) + jj] >= x_b, 1, 0)
            return tuple(cnt)
        zero = jnp.zeros((tq, LANES), I32)
        c0, c1 = for_chunks(body, (zero, zero))
        return _lane_bcast(_lane_sum(c0 + c1))

    def max_below(x_b):
        def body(k0, m):
            for jj in range(nsub):
                kk = score_scr[tile_of(k0) + jj]
                m = jnp.maximum(m, jnp.where(kk < x_b, kk, -jnp.inf))
            return m
        return _lane_bcast(_lane_max(for_chunks(body, floor)))

    def count_tie_below(x_b):
        def body(k0, cnt):
            for jj in range(nsub):
                kk = score_scr[tile_of(k0) + jj]
                cnt = cnt + jnp.where((kk == thr_b) & (k0 + jj * LANES + lane < x_b), 1, 0)
            return cnt
        return _lane_bcast(_lane_sum(for_chunks(body, jnp.zeros((tq, LANES), I32))))

    lo0, hi0 = _threshold_bracket(g_even, g_odd)
    thr_b, tie, n_tie = _select_threshold(count_ge, max_below, _lane_bcast(lo0), _lane_bcast(hi0),
                                          qpos + 1 <= topk, topk)
    cut_scr[...] = jnp.full((tq, LANES), INT_BIG, I32)

    @pl.when(jnp.max(tie.astype(I32)) > 0)
    def _():
        cut_scr[...] = _tie_cut(count_tie_below, tie, n_tie, nbits)
    cut_b = cut_scr[...]

    def sel_tiles(k0):
        out = []
        for jj in range(nsub):
            kk = score_scr[tile_of(k0) + jj]
            sel = _selected(kk, k0 + jj * LANES + lane, thr_b, cut_b)
            out.append(jnp.concatenate([sel] * REP, axis=0))
        return out

    m_scr[...] = jnp.full(m_scr.shape, -jnp.inf, F32)

    def max_chunk(k0, carry):
        sels = sel_tiles(k0)
        for g in range(N_KV_HEADS):
            q2 = jnp.concatenate([q_ref[REP * g + r] for r in range(REP)], axis=0)
            s = _dot(q2, kt_ref[HEAD_DIM * g:HEAD_DIM * (g + 1), pl.ds(k0, ck)])
            m = m_scr[g]
            for jj in range(nsub):
                m = jnp.maximum(m, jnp.where(sels[jj], s[:, jj * LANES:(jj + 1) * LANES], -jnp.inf))
            m_scr[g] = m
        return carry
    for_chunks(max_chunk, 0)

    lane2 = lax.broadcasted_iota(I32, (REP * tq, LANES), 1)
    for g in range(N_KV_HEADS):
        q2 = jnp.concatenate([q_ref[REP * g + r] for r in range(REP)], axis=0).astype(F32)
        q2 = jnp.concatenate([q2, jnp.zeros((REP * tq, LANES - HEAD_DIM), F32)], axis=1)
        neg_m = -_lane_bcast(_lane_max(m_scr[g]))
        qa_scr[g] = jnp.where(lane2 == HEAD_DIM, neg_m, q2).astype(BF16)
    ones_row = jnp.where(lax.broadcasted_iota(I32, (LANES - HEAD_DIM, ck), 0) == 0, 1.0, 0.0).astype(BF16)

    acc_scr[...] = jnp.zeros(acc_scr.shape, F32)

    def pv_chunk(k0, carry):
        sels = sel_tiles(k0)
        for g in range(N_KV_HEADS):
            k_aug = jnp.concatenate([kt_ref[HEAD_DIM * g:HEAD_DIM * (g + 1), pl.ds(k0, ck)], ones_row], axis=0)
            s = _dot(qa_scr[g], k_aug)
            ps = [jnp.where(sels[jj], jnp.exp2(s[:, jj * LANES:(jj + 1) * LANES]), 0.0).astype(BF16)
                  for jj in range(nsub)]
            acc_scr[g] += _dot(jnp.concatenate(ps, axis=1), va_ref[pl.ds(k0, ck), LANES * g:LANES * (g + 1)])
        return carry
    for_chunks(pv_chunk, 0)

    outs = []
    for g in range(N_KV_HEADS):
        acc = acc_scr[g]
        o = acc[:, :HEAD_DIM] / acc[:, HEAD_DIM:HEAD_DIM + 1]
        outs += [o[r * tq:(r + 1) * tq] for r in range(REP)]
    o_ref[...] = jnp.concatenate(outs, axis=1).astype(BF16)


def _attn_prompt(iq8, q8, iks, ikt, kt, va, topk):
    n = iks.shape[0]
    tq = min(128, n)
    ck = min(256, n // CHUNKS_PER_STEP)
    nbits = max(1, int(np.ceil(np.log2(n))))
    return pl.pallas_call(
        functools.partial(_attn_prompt_kernel, tq=tq, ck=ck, topk=topk, nbits=nbits),
        grid=(n // tq,),
        in_specs=[pl.BlockSpec((IDX_HEADS, tq, IDX_DIM), lambda i: (0, i, 0)),
                  pl.BlockSpec((N_HEADS, tq, HEAD_DIM), lambda i: (0, i, 0)),
                  pl.BlockSpec((tq, LANES), lambda i: (i, 0)),
                  _resident((IDX_DIM, n)), _resident((KV_W, n)), _resident((n, N_KV_HEADS * LANES))],
        out_specs=pl.BlockSpec((tq, ATTN_W), lambda i: (i, 0)),
        out_shape=jax.ShapeDtypeStruct((n, ATTN_W), BF16),
        scratch_shapes=[pltpu.VMEM((n // LANES, tq, LANES), F32),
                        pltpu.VMEM((IDX_HEADS, tq, LANES), F32),
                        pltpu.VMEM((tq, LANES), I32),
                        pltpu.VMEM((N_KV_HEADS, REP * tq, LANES), F32),
                        pltpu.VMEM((N_KV_HEADS, REP * tq, LANES), BF16),
                        pltpu.VMEM((N_KV_HEADS, REP * tq, LANES), F32)],
        compiler_params=_params("arbitrary"),
        name="attn_prompt",
    )(iq8, q8, iks, ikt, kt, va)


def _attn_sample_kernel(pt_ref, iq_ref, q_ref, iks_ref, kbn_ref, vbn_ref, cik_hbm, ck_hbm, cv_hbm, o_ref,
                        ikp, kp, vp, sems, score_scr, cut_scr, *, n_pages, t_new, ppc, topk, nbits):
    b = pl.program_id(0)
    past = n_pages * PAGE_SIZE
    rows = IDX_HEADS * t_new
    n_ck = n_pages // ppc
    ck = ppc * PAGE_SIZE

    def page_copy(src, dst, sem, p, page):
        return pltpu.make_async_copy(src.at[page], dst.at[p], sem)

    def start_pages(src, dst, sem):
        def body(p, carry):
            page_copy(src, dst, sem, p, pt_ref[b, p]).start()
            return carry
        lax.fori_loop(0, n_pages, body, 0)
    start_pages(cik_hbm, ikp, sems.at[0])
    start_pages(ck_hbm, kp, sems.at[1])
    start_pages(cv_hbm, vp, sems.at[2])

    def wait_pages(src, dst, sem):
        def body(p, carry):
            page_copy(src, dst, sem, p, 0).wait()
            return carry
        lax.fori_loop(0, n_pages, body, 0)

    def chunk_of(buf, c, width):
        return jnp.concatenate([buf[c * ppc + j].reshape(width, PAGE_SIZE) for j in range(ppc)], axis=1).astype(BF16)

    iks = iks_ref[...]
    w = iks[:, IW_LANE:IW_LANE + IDX_HEADS] * (IDX_HEADS ** -0.5)
    wb = _lane_bcast(jnp.concatenate([w[:, h:h + 1] for h in range(IDX_HEADS)], axis=0))
    iq = iq_ref[0].reshape(rows, IDX_DIM)
    qrow = lax.broadcasted_iota(I32, (t_new, LANES), 0)
    lane = lax.broadcasted_iota(I32, (t_new, LANES), 1)

    def head_sum(t):
        acc = t[0:t_new]
        for h in range(1, IDX_HEADS):
            acc = acc + t[h * t_new:(h + 1) * t_new]
        return acc

    wait_pages(cik_hbm, ikp, sems.at[0])

    def score_chunk(c, gmax):
        s = _dot(iq, chunk_of(ikp, c, IDX_DIM))
        gmax = list(gmax)
        for jj in range(ppc):
            t = jnp.maximum(s[:, jj * LANES:(jj + 1) * LANES], 0.0) * wb
            kj = head_sum(t)
            score_scr[:, pl.ds(pl.multiple_of(c * ck, ck) + jj * LANES, LANES)] = kj
            gmax[jj % 2] = jnp.maximum(gmax[jj % 2], kj)
        return tuple(gmax)
    floor = jnp.full((t_new, LANES), -jnp.inf, F32)
    g_even, g_odd = lax.fori_loop(0, n_ck, score_chunk, (floor, floor))

    pad_rows = lambda a: jnp.concatenate([a, jnp.zeros((LANES - t_new, a.shape[1]), a.dtype)], axis=0)
    new_ok = (lane <= qrow) & (lane < t_new)
    s_new = jnp.maximum(_dot_t(iq, pad_rows(iks[:, :IDX_DIM].astype(BF16))), 0.0) * wb
    k_new = jnp.where(new_ok, head_sum(s_new), -jnp.inf)
    score_scr[:, past:past + LANES] = k_new

    n_tiles = past // LANES + 1
    tile_idx = lambda jt: jt * LANES + lane

    def count_over(pred):
        def run(cand):
            cand_b = _lane_bcast(cand)
            cnt = [jnp.zeros((t_new, LANES), I32)] * 2
            for jt in range(n_tiles):
                kk = score_scr[:, jt * LANES:(jt + 1) * LANES]
                cnt[jt % 2] = cnt[jt % 2] + jnp.where(pred(kk, tile_idx(jt), cand_b), 1, 0)
            return _lane_sum(cnt[0] + cnt[1])
        return run

    def max_below(x):
        x_b = _lane_bcast(x)
        m = floor
        for jt in range(n_tiles):
            kk = score_scr[:, jt * LANES:(jt + 1) * LANES]
            m = jnp.maximum(m, jnp.where(kk < x_b, kk, -jnp.inf))
        return _lane_max(m)

    lo0, hi0 = _threshold_bracket(jnp.maximum(g_even, k_new), g_odd)
    done0 = jnp.zeros((t_new, 1), jnp.bool_)
    thr, tie, n_tie = _select_threshold(count_over(lambda kk, kidx, cb: kk >= cb), max_below, lo0, hi0, done0, topk)
    thr_b = _lane_bcast(thr)
    cut_scr[...] = jnp.full((t_new, LANES), INT_BIG, I32)

    @pl.when(jnp.max(tie.astype(I32)) > 0)
    def _():
        cut = _tie_cut(count_over(lambda kk, kidx, cb: (kk == thr_b) & (kidx < cb)), tie, n_tie, nbits)
        cut_scr[...] = _lane_bcast(cut)
    cut_b = cut_scr[...]

    def sel_rows(kk, kidx):
        return jnp.concatenate([_selected(kk, kidx, thr_b, cut_b)] * N_HEADS, axis=0)

    def sel_chunk(c, jj):
        kk = score_scr[:, pl.ds(pl.multiple_of(c * ck, ck) + jj * LANES, LANES)]
        return sel_rows(kk, c * ck + jj * LANES + lane)

    wait_pages(ck_hbm, kp, sems.at[1])
    wait_pages(cv_hbm, vp, sems.at[2])
    q = q_ref[0].reshape(rows, KV_W)
    s_new = _dot_t(q, pad_rows(kbn_ref[...]))
    sel_new = sel_rows(k_new, tile_idx(past // LANES))

    def max_chunk(c, m):
        s = _dot(q, chunk_of(kp, c, KV_W))
        for jj in range(ppc):
            m = jnp.maximum(m, jnp.where(sel_chunk(c, jj), s[:, jj * LANES:(jj + 1) * LANES], -jnp.inf))
        return m
    m = lax.fori_loop(0, n_ck, max_chunk, jnp.where(sel_new, s_new, -jnp.inf))
    m_b = _lane_bcast(_lane_max(m))

    p_new = jnp.where(sel_new, jnp.exp2(s_new - m_b), 0.0)
    acc0 = _dot(p_new.astype(BF16), pad_rows(vbn_ref[...]))

    def pv_chunk(c, carry):
        l, acc = carry
        s = _dot(q, chunk_of(kp, c, KV_W))
        ps = []
        for jj in range(ppc):
            p = jnp.where(sel_chunk(c, jj), jnp.exp2(s[:, jj * LANES:(jj + 1) * LANES] - m_b), 0.0)
            l = l + p
            ps.append(p.astype(BF16))
        return l, acc + _dot_t(jnp.concatenate(ps, axis=1), chunk_of(vp, c, KV_W))
    l, acc = lax.fori_loop(0, n_ck, pv_chunk, (p_new, acc0))
    o = acc / _lane_sum(l)
    o_ref[0] = o.reshape(N_HEADS, t_new, KV_W).astype(BF16)


def _attn_sample(page_table, iq_s, q_s, iks_s, kb_s, vb_s, cache_ikt, cache_kt, cache_vt, topk):
    db, n_pages = page_table.shape
    t_new = iq_s.shape[2]
    past = n_pages * PAGE_SIZE
    assert past >= 2 * LANES and past + 1 > topk, "the threshold bracket needs two full key tiles"
    ppc = 32 if n_pages % 32 == 0 else 2
    nbits = max(1, int(np.ceil(np.log2(past + LANES))))
    grid_spec = pltpu.PrefetchScalarGridSpec(
        num_scalar_prefetch=1,
        grid=(db,),
        in_specs=[pl.BlockSpec((1, IDX_HEADS, t_new, IDX_DIM), lambda b, pt: (b, 0, 0, 0)),
                  pl.BlockSpec((1, N_HEADS, t_new, KV_W), lambda b, pt: (b, 0, 0, 0)),
                  pl.BlockSpec((t_new, LANES), lambda b, pt: (b, 0)),
                  pl.BlockSpec((t_new, KV_W), lambda b, pt: (b, 0)),
                  pl.BlockSpec((t_new, KV_W), lambda b, pt: (b, 0)),
                  pl.BlockSpec(memory_space=pl.ANY), pl.BlockSpec(memory_space=pl.ANY),
                  pl.BlockSpec(memory_space=pl.ANY)],
        out_specs=pl.BlockSpec((1, N_HEADS, t_new, KV_W), lambda b, pt: (b, 0, 0, 0)),
        scratch_shapes=[pltpu.VMEM((n_pages, IDX_DIM, PAGE_SIZE), F32),
                        pltpu.VMEM((n_pages, N_KV_HEADS, HEAD_DIM, PAGE_SIZE), F32),
                        pltpu.VMEM((n_pages, N_KV_HEADS, HEAD_DIM, PAGE_SIZE), F32),
                        pltpu.SemaphoreType.DMA((3,)),
                        pltpu.VMEM((t_new, past + LANES), F32),
                        pltpu.VMEM((t_new, LANES), I32)])
    return pl.pallas_call(
        functools.partial(_attn_sample_kernel, n_pages=n_pages, t_new=t_new, ppc=ppc, topk=topk, nbits=nbits),
        grid_spec=grid_spec,
        out_shape=jax.ShapeDtypeStruct((db, N_HEADS, t_new, KV_W), BF16),
        compiler_params=_params("arbitrary"),
        name="attn_sample",
    )(page_table, iq_s, q_s, iks_s, kb_s, vb_s, cache_ikt, cache_kt, cache_vt)


def _mix_kernel(attn_ref, sga_ref, yc_ref, x_ref, g1_ref, sh2_ref, sc2_ref, n2_ref, woa_ref, wout_ref,
                wr_hi_ref, wr_lo_ref, br_ref, x1_ref, h2_ref, comb_ref, *, sample, tm):
    if sample:
        a = attn_ref[...]
        y_attn = jnp.zeros((tm, D_MODEL), F32)
        for h in range(N_HEADS):
            y_attn = y_attn + _dot(a[:, h].reshape(tm, KV_W), woa_ref[h])
    else:
        y_attn = _dot(attn_ref[...], woa_ref[...])
    mixed = _dot((sga_ref[...] * y_attn + yc_ref[...]).astype(BF16), wout_ref[...])
    x3 = x_ref[...]
    g, r, _ = x3.shape
    x1 = x3 + g1_ref[...] * mixed.reshape(g, r, D_MODEL)
    x1_ref[...] = x1
    h2 = _modnorm(x1, n2_ref[...], sc2_ref[...], sh2_ref[...])
    h2_ref[...] = h2.astype(BF16)

    lg = _dot3(h2, wr_hi_ref[...], wr_lo_ref[...]) + br_ref[...]
    lane = lax.broadcasted_iota(I32, (tm, LANES), 1)
    is_g = lane < N_GROUPS
    is_e = (lane >= ROUTER_E0) & (lane < ROUTER_E0 + N_EXPERTS)
    first = lambda msk: _lane_min(jnp.where(msk, lane, LANES))
    gl = jnp.where(is_g, lg, -jnp.inf)
    gmax = _lane_max(gl)
    g_sel = first(gl == gmax)
    g_w = 1.0 / _lane_sum(jnp.where(is_g, jnp.exp(lg - gmax), 0.0))
    in_grp = is_e & (((lane - ROUTER_E0) >> 2) == g_sel)
    me = jnp.where(in_grp, lg, -jnp.inf)
    v1 = _lane_max(me)
    i1 = first(in_grp & (me == v1))
    rest = in_grp & (lane != i1)
    me2 = jnp.where(rest, lg, -jnp.inf)
    v2 = _lane_max(me2)
    i2 = first(rest & (me2 == v2))
    e2 = jnp.exp(v2 - v1)
    den = 1.0 + e2
    comb_ref[...] = jnp.where(lane == i1, g_w / den, 0.0) + jnp.where(lane == i2, g_w * e2 / den, 0.0)


def _mix(attn, sga, yc, x3, mod3, mod_row0, norm2_g, w_oa, w_out, wr_hi, wr_lo, b_r, *, sample):
    G, R, _ = x3.shape
    n = G * R
    if sample:
        bb = min(G, 32)
        tm = bb * R
        nb = G // bb
        xspec = pl.BlockSpec((bb, R, D_MODEL), lambda i: (i, 0, 0))
        mspec = lambda c: pl.BlockSpec((bb, 1, D_MODEL), lambda i: (i, 0, c))
        aspec = pl.BlockSpec((bb, N_HEADS, R, KV_W), lambda i: (i, 0, 0, 0))
        woa_spec = _resident((N_HEADS, KV_W, D_MODEL))
    else:
        tm = min(R, 256)
        nb = R // tm
        xspec = pl.BlockSpec((1, tm, D_MODEL), lambda i: (0, i, 0))
        mspec = lambda c: pl.BlockSpec((1, 1, D_MODEL), lambda i: (mod_row0, 0, c))
        aspec = pl.BlockSpec((tm, ATTN_W), lambda i: (i, 0))
        woa_spec = _resident((ATTN_W, D_MODEL))
    row = lambda w: pl.BlockSpec((tm, w), lambda i: (i, 0))
    return pl.pallas_call(
        functools.partial(_mix_kernel, sample=sample, tm=tm),
        grid=(nb,),
        in_specs=[aspec, row(D_MODEL), row(D_MODEL), xspec, mspec(2), mspec(3), mspec(4), _resident((1, D_MODEL)),
                  woa_spec, _resident((D_MODEL, D_MODEL)), _resident((D_MODEL, LANES)), _resident((D_MODEL, LANES)),
                  _resident((1, LANES))],
        out_specs=[xspec, row(D_MODEL), row(LANES)],
        out_shape=[jax.ShapeDtypeStruct((G, R, D_MODEL), F32), jax.ShapeDtypeStruct((n, D_MODEL), BF16),
                   jax.ShapeDtypeStruct((n, LANES), F32)],
        compiler_params=_params("arbitrary"),
        name="mix_sample" if sample else "mix_prompt",
    )(attn, sga, yc, x3, mod3, mod3, mod3, norm2_g.reshape(1, D_MODEL), w_oa, w_out, wr_hi, wr_lo, b_r)


def _moe_kernel(h2_ref, comb_ref, x1_ref, g2_ref, fg_ref, weg_ref, weu_ref, wed_ref, y_ref, acc_scr, *, tm, epb):
    j = pl.program_id(1)

    @pl.when(j == 0)
    def _():
        acc_scr[...] = jnp.zeros(acc_scr.shape, F32)

    h2 = h2_ref[...]
    a = _dot(h2, weg_ref[...])
    u = _dot(h2, weu_ref[...])
    hid = a * _sigmoid(a) * u
    comb = comb_ref[...]
    lane = lax.broadcasted_iota(I32, (tm, LANES), 1)
    parts = []
    for e in range(epb):
        ce = _lane_sum(jnp.where(lane == ROUTER_E0 + j * epb + e, comb, 0.0))
        parts.append((hid[:, e * EXPERT_FF:(e + 1) * EXPERT_FF] * ce).astype(BF16))
    acc_scr[...] += _dot(jnp.concatenate(parts, axis=1), wed_ref[...])

    @pl.when(j == pl.num_programs(1) - 1)
    def _():
        x1 = x1_ref[...]
        g, r, _ = x1.shape
        x2 = x1 + g2_ref[...] * acc_scr[...].reshape(g, r, D_MODEL)
        ms = jnp.mean(x2 * x2, axis=-1, keepdims=True)
        y_ref[...] = x2 * lax.rsqrt(ms + EPS) * fg_ref[...][None]


def _moe(h2, comb, x1, mod3, mod_row0, final_g, w_eg, w_eu, w_ed, *, sample):
    G, R, _ = x1.shape
    epb = EXP_PER_GROUP
    nj = N_EXPERTS // epb
    wblk = epb * EXPERT_FF
    if sample:
        bb = min(G, 64)
        tm = bb * R
        nb = G // bb
        xspec = pl.BlockSpec((bb, R, D_MODEL), lambda i, j: (i, 0, 0))
        g2spec = pl.BlockSpec((bb, 1, D_MODEL), lambda i, j: (i, 0, 5))
    else:
        tm = min(R, 512)
        nb = R // tm
        xspec = pl.BlockSpec((1, tm, D_MODEL), lambda i, j: (0, i, 0))
        g2spec = pl.BlockSpec((1, 1, D_MODEL), lambda i, j: (mod_row0, 0, 5))
    return pl.pallas_call(
        functools.partial(_moe_kernel, tm=tm, epb=epb),
        grid=(nb, nj),
        in_specs=[pl.BlockSpec((tm, D_MODEL), lambda i, j: (i, 0)),
                  pl.BlockSpec((tm, LANES), lambda i, j: (i, 0)),
                  xspec, g2spec,
                  pl.BlockSpec((1, D_MODEL), lambda i, j: (0, 0)),
                  pl.BlockSpec((D_MODEL, wblk), lambda i, j: (0, j)),
                  pl.BlockSpec((D_MODEL, wblk), lambda i, j: (0, j)),
                  pl.BlockSpec((wblk, D_MODEL), lambda i, j: (j, 0))],
        out_specs=xspec,
        out_shape=jax.ShapeDtypeStruct((G, R, D_MODEL), F32),
        scratch_shapes=[pltpu.VMEM((tm, D_MODEL), F32)],
        compiler_params=_params("arbitrary", "arbitrary"),
        name="moe_sample" if sample else "moe_prompt",
    )(h2, comb, x1, mod3, final_g.reshape(1, D_MODEL), w_eg, w_eu, w_ed)


def kernel(x_prompt, x_sample, cache_k, cache_v, cache_idx_k, state_conv, page_table, c_prompt, c_sample,
           w_ada, b_ada, norm1_g, norm2_g, w_in_mix, conv_w, w_o_attn, w_o_conv, w_out,
           w_router_group, b_router_group, w_router_expert, b_router_expert,
           w_exp_gate, w_exp_up, w_exp_down, final_norm_g):
    depth = w_ada.shape[0]
    assert depth == 1, "single-layer step"
    B, S, _ = x_prompt.shape
    assert B == 1, "one prompt sequence"
    DB, T, _ = x_sample.shape
    assert T == SUBLANES, "sample tokens per request must fill one sublane tile"
    n_pages = page_table.shape[1]
    past = n_pages * PAGE_SIZE

    n_rows = DB + B
    pad = (-n_rows) % SUBLANES
    c_all = jnp.concatenate([c_sample, c_prompt, jnp.zeros((pad, D_MODEL), F32)], axis=0)
    mod = _ada(c_all, w_ada[0], b_ada[0])
    mod3 = mod.reshape(n_rows + pad, 1, 6 * D_MODEL)

    half = HEAD_DIM // 8
    inv_freq = ROPE_THETA ** (-jnp.arange(half, dtype=F32) / half)
    lane = np.arange(LANES)
    invf = jnp.where((lane % HEAD_DIM) < 2 * half, inv_freq[lane % half], 0.0).reshape(1, LANES).astype(F32)
    w_in = w_in_mix[0]
    w_oc = w_o_conv[0].astype(BF16)
    w_oa = w_o_attn[0]
    w_oa_p = w_oa.astype(BF16)
    w_oa_s = jnp.stack([
        jnp.concatenate([jnp.zeros((HEAD_DIM * (h // REP), D_MODEL), F32), w_oa[HEAD_DIM * h:HEAD_DIM * (h + 1)],
                         jnp.zeros((KV_W - HEAD_DIM * (h // REP + 1), D_MODEL), F32)], axis=0)
        for h in range(N_HEADS)]).astype(BF16)
    w_o = w_out[0].astype(BF16)
    zr = jnp.zeros((D_MODEL, LANES - ROUTER_E0 - N_EXPERTS), F32)
    w_r = jnp.concatenate([w_router_group[0], w_router_expert[0], zr], axis=1)
    wr_hi = w_r.astype(BF16)
    wr_lo = (w_r - wr_hi.astype(F32)).astype(BF16)
    b_r = jnp.concatenate([b_router_group[0], b_router_expert[0],
                           jnp.zeros((LANES - ROUTER_E0 - N_EXPERTS,), F32)]).reshape(1, LANES)
    cat_e = lambda w: jnp.transpose(w, (1, 0, 2)).reshape(D_MODEL, N_EXPERTS * EXPERT_FF).astype(BF16)
    w_eg = cat_e(w_exp_gate[0])
    w_eu = cat_e(w_exp_up[0])
    w_ed = w_exp_down[0].reshape(N_EXPERTS * EXPERT_FF, D_MODEL).astype(BF16)

    xp3 = x_prompt
    (q8, iq8, iks_p, k_p, v_p, kt, ikt, va_p, yc_p, sga_p, cst_p) = _proj(
        xp3, mod3, DB, norm1_g[0], _proj_weight(w_in, False), invf, conv_w[0], w_oc, None, sample=False, pos_base=0)
    attn_p = _attn_prompt(iq8, q8, iks_p, ikt, kt, va_p, min(TOPK_MAX, S // 4))
    x1_p, h2_p, comb_p = _mix(attn_p, sga_p, yc_p, xp3, mod3, DB, norm2_g[0], w_oa_p, w_o, wr_hi, wr_lo, b_r,
                              sample=False)
    y_p = _moe(h2_p, comb_p, x1_p, mod3, DB, final_norm_g, w_eg, w_eu, w_ed, sample=False)

    cache_ikt = jnp.transpose(cache_idx_k[0], (0, 2, 1))
    cache_kt = jnp.transpose(cache_k[0], (0, 2, 3, 1))
    cache_vt = jnp.transpose(cache_v[0], (0, 2, 3, 1))
    (q_s, iq_s, iks_s, k_s, v_s, kb_s, vb_s, yc_s, sga_s, cst_s) = _proj(
        x_sample, mod3, 0, norm1_g[0], _proj_weight(w_in, True), invf, conv_w[0], w_oc, state_conv[0],
        sample=True, pos_base=past)
    attn_s = _attn_sample(page_table, iq_s, q_s, iks_s, kb_s, vb_s, cache_ikt, cache_kt, cache_vt,
                          min(TOPK_MAX, (past + T) // 4))
    x1_s, h2_s, comb_s = _mix(attn_s, sga_s, yc_s, x_sample, mod3, 0, norm2_g[0], w_oa_s, w_o, wr_hi, wr_lo, b_r,
                              sample=True)
    y_s = _moe(h2_s, comb_s, x1_s, mod3, 0, final_norm_g, w_eg, w_eu, w_ed, sample=True)

    return (y_p, y_s,
            k_p.reshape(1, B, S, N_KV_HEADS, HEAD_DIM), v_p.reshape(1, B, S, N_KV_HEADS, HEAD_DIM),
            iks_p[:, :IDX_DIM].reshape(1, B, S, IDX_DIM),
            cst_p[SUBLANES - (CONV_K - 1):].reshape(1, B, CONV_K - 1, CONV_W),
            k_s.reshape(1, DB, T, N_KV_HEADS, HEAD_DIM), v_s.reshape(1, DB, T, N_KV_HEADS, HEAD_DIM),
            iks_s[:, :IDX_DIM].reshape(1, DB, T, IDX_DIM),
            cst_s[:, T - (CONV_K - 1):].reshape(1, DB, CONV_K - 1, CONV_W))
```

```python
import functools

import jax
import jax.numpy as jnp
import numpy as np
from jax import lax
from jax.experimental import pallas as pl
from jax.experimental.pallas import tpu as pltpu

F32 = jnp.float32
BF16 = jnp.bfloat16
I32 = jnp.int32

D_MODEL = 1024
N_HEADS = 8
N_KV_HEADS = 4
HEAD_DIM = 64
ATTN_W = N_HEADS * HEAD_DIM
KV_W = N_KV_HEADS * HEAD_DIM
ROPE_THETA = 500000.0
IDX_HEADS = 8
IDX_DIM = 64
TOPK_MAX = 256
CONV_W = 512
CONV_K = 3
N_GROUPS = 4
EXP_PER_GROUP = 4
N_EXPERTS = N_GROUPS * EXP_PER_GROUP
EXPERT_FF = 256
EPS = 1e-6
PAGE_SIZE = 128

LANES = 128
SUBLANES = 8
VMEM_LIMIT = 56 * 1024 * 1024
INT_BIG = 2 ** 30
BIG_SCORE = 3.0e38
SHIFT_MARGIN = 1.0 + 2.0 ** -6
MIN_ROW_SUM = 2.0 ** -60
CHUNKS_PER_STEP = 4
BISECT_STEPS = 13
LOG2E = 1.4426950408889634
Q_SCALE = HEAD_DIM ** -0.5 * LOG2E
IW_LANE = 112
ROUTER_E0 = 4
REP = N_HEADS // N_KV_HEADS


def _dot(a, b):
    return jnp.dot(a, b, preferred_element_type=F32)


def _dot_t(a, b):
    return lax.dot_general(a, b, (((1,), (1,)), ((), ())), preferred_element_type=F32)


def _split_bf16(a):
    hi = a.astype(BF16)
    lo = (a - hi.astype(F32)).astype(BF16)
    return hi, lo


def _dot3(a, w_hi, w_lo):
    a_hi, a_lo = _split_bf16(a)
    return _dot(a_hi, w_hi) + (_dot(a_hi, w_lo) + _dot(a_lo, w_hi))


def _sigmoid(x):
    return 1.0 / (1.0 + jnp.exp(-x))


def _params(*sem):
    return pltpu.CompilerParams(dimension_semantics=sem, vmem_limit_bytes=VMEM_LIMIT)


def _resident(shape):
    nd = len(shape)
    return pl.BlockSpec(shape, lambda *_: (0,) * nd, pipeline_mode=pl.Buffered(1))


def _ada_kernel(c_ref, w_ref, b_ref, o_ref):
    c = c_ref[...]
    a = c * _sigmoid(c)
    w = w_ref[...]
    w_hi, w_lo = _split_bf16(w)
    o_ref[...] = _dot3(a, w_hi, w_lo) + b_ref[...]


def _ada(c_all, w_ada, b_ada):
    rows = c_all.shape[0]
    n = w_ada.shape[1]
    tn = 512
    return pl.pallas_call(
        _ada_kernel,
        grid=(n // tn,),
        in_specs=[pl.BlockSpec((rows, D_MODEL), lambda j: (0, 0)),
                  pl.BlockSpec((D_MODEL, tn), lambda j: (0, j)),
                  pl.BlockSpec((1, tn), lambda j: (0, j))],
        out_specs=pl.BlockSpec((rows, tn), lambda j: (0, j)),
        out_shape=jax.ShapeDtypeStruct((rows, n), F32),
        compiler_params=_params("arbitrary"),
        name="ada",
    )(c_all, w_ada, b_ada.reshape(1, n))


def _modnorm(x3, g, sc3, sh3):
    ms = jnp.mean(x3 * x3, axis=-1, keepdims=True)
    xn = x3 * lax.rsqrt(ms + EPS)
    h = xn * g[None] * (1.0 + sc3) + sh3
    return h.reshape(x3.shape[0] * x3.shape[1], x3.shape[2])


def _proj_kernel(*refs, sample, tm, pos_base, cols):
    if sample:
        (x_ref, sh_ref, sc_ref, g_ref, w_ref, invf_ref, cw_ref, woc_ref, st_ref,
         qo_ref, iqo_ref, iks_ref, k_ref, v_ref, kb_ref, vb_ref, yc_ref, sga_ref, cst_ref) = refs
    else:
        (x_ref, sh_ref, sc_ref, g_ref, w_ref, invf_ref, cw_ref, woc_ref,
         qo_ref, iqo_ref, iks_ref, k_ref, v_ref, kt_ref, ikt_ref, vb_ref, yc_ref, sga_ref, cst_ref,
         cu_scr) = refs
    i = pl.program_id(0)
    hb = _modnorm(x_ref[...], g_ref[...], sc_ref[...], sh_ref[...]).astype(BF16)

    def proj(name):
        c0, c1 = cols[name]
        return _dot(hb, w_ref[:, c0:c1])

    lane = lax.broadcasted_iota(I32, (tm, LANES), 1)
    row = lax.broadcasted_iota(I32, (tm, LANES), 0)
    if sample:
        pos = pos_base + (row & (SUBLANES - 1))
    else:
        pos = i * tm + row
    ang = pos.astype(F32) * invf_ref[...]
    cosv = jnp.cos(ang)
    sinv = jnp.sin(ang)
    j = lane & (HEAD_DIM - 1)
    half = HEAD_DIM // 8
    rc = jnp.where(j < 2 * half, cosv, 1.0)
    rs_lo = jnp.where((j >= half) & (j < 2 * half), sinv, 0.0)
    rs_hi = jnp.where(j < half, -sinv, 0.0)

    def rope(xs):
        return xs * rc + pltpu.roll(xs, half, 1) * rs_lo + pltpu.roll(xs, LANES - half, 1) * rs_hi

    pq = proj("q")
    if sample:
        for h in range(N_HEADS):
            blk = pq[:, KV_W * h:KV_W * (h + 1)]
            r = jnp.concatenate([rope(blk[:, :LANES]), rope(blk[:, LANES:])], axis=1) * Q_SCALE
            qo_ref[:, h] = r.reshape(tm // SUBLANES, SUBLANES, KV_W).astype(BF16)
    else:
        for s in range(ATTN_W // LANES):
            r = (rope(pq[:, LANES * s:LANES * (s + 1)]) * Q_SCALE).astype(BF16)
            qo_ref[2 * s] = r[:, :HEAD_DIM]
            qo_ref[2 * s + 1] = r[:, HEAD_DIM:]

    pk = proj("k")
    for s in range(KV_W // LANES):
        rk = rope(pk[:, LANES * s:LANES * (s + 1)])
        k_ref[:, LANES * s:LANES * (s + 1)] = rk
        if sample:
            kb_ref[:, LANES * s:LANES * (s + 1)] = rk.astype(BF16)
        else:
            kt_ref[LANES * s:LANES * (s + 1), :] = rk.T.astype(BF16)

    pv = proj("v")
    v_ref[...] = pv
    if sample:
        vb_ref[...] = pv.astype(BF16)
    else:
        tail = jnp.where(lane == HEAD_DIM, 1.0, 0.0)
        for s in range(KV_W // LANES):
            slab = pv[:, LANES * s:LANES * (s + 1)]
            vb_ref[:, LANES * 2 * s:LANES * (2 * s + 1)] = jnp.where(lane < HEAD_DIM, slab, tail).astype(BF16)
            vb_ref[:, LANES * (2 * s + 1):LANES * (2 * s + 2)] = jnp.where(
                lane < HEAD_DIM, pltpu.roll(slab, HEAD_DIM, 1), tail).astype(BF16)

    piq = proj("iq")
    for s in range(IDX_HEADS * IDX_DIM // LANES):
        r = (rope(piq[:, LANES * s:LANES * (s + 1)]) * (IDX_DIM ** -0.5)).astype(BF16)
        if sample:
            iqo_ref[:, 2 * s] = r[:, :IDX_DIM].reshape(tm // SUBLANES, SUBLANES, IDX_DIM)
            iqo_ref[:, 2 * s + 1] = r[:, IDX_DIM:].reshape(tm // SUBLANES, SUBLANES, IDX_DIM)
        else:
            iqo_ref[2 * s] = r[:, :IDX_DIM]
            iqo_ref[2 * s + 1] = r[:, IDX_DIM:]

    slab = rope(proj("iks"))
    iks_ref[...] = slab
    if not sample:
        ikt_ref[...] = slab.T[:IDX_DIM, :].astype(BF16)

    cu = proj("gc") * proj("u")
    cw = cw_ref[...]
    if sample:
        g8 = tm // SUBLANES
        st = st_ref[...]
        grp = (g8, SUBLANES, CONV_W)
        s0 = jnp.broadcast_to(st[:, 0:1, :], grp).reshape(tm, CONV_W)
        s1 = jnp.broadcast_to(st[:, 1:2, :], grp).reshape(tm, CONV_W)
        t = lax.broadcasted_iota(I32, (tm, CONV_W), 0) & (SUBLANES - 1)
        m1 = jnp.where(t >= 1, pltpu.roll(cu, 1, 0), s1)
        m2 = jnp.where(t >= 2, pltpu.roll(cu, 2, 0), jnp.where(t == 1, s1, s0))
        conv = m2 * cw[0:1] + m1 * cw[1:2] + cu * cw[2:3]
        cst_ref[...] = cu.reshape(grp)
    else:
        @pl.when(i == 0)
        def _():
            cu_scr[0:SUBLANES, :] = jnp.zeros((SUBLANES, CONV_W), F32)
        cu_scr[SUBLANES:SUBLANES + tm, :] = cu
        m1 = cu_scr[SUBLANES - 1:SUBLANES - 1 + tm, :]
        m2 = cu_scr[SUBLANES - 2:SUBLANES - 2 + tm, :]
        conv = m2 * cw[0:1] + m1 * cw[1:2] + cu * cw[2:3]
        tail = cu_scr[tm:tm + SUBLANES, :]
        cu_scr[0:SUBLANES, :] = tail
        cst_ref[...] = tail
    zc = (proj("gb_conv") * conv).astype(BF16)
    yconv = _dot(zc, woc_ref[...])
    sga_ref[...] = _sigmoid(proj("ga"))
    yc_ref[...] = _sigmoid(proj("gb")) * yconv


def _proj_layout(sample):
    names = [("q", N_HEADS * KV_W if sample else ATTN_W), ("k", KV_W), ("v", KV_W),
             ("iq", IDX_HEADS * IDX_DIM), ("iks", LANES), ("u", CONV_W), ("gb_conv", CONV_W), ("gc", CONV_W),
             ("ga", D_MODEL), ("gb", D_MODEL)]
    cols, c = {}, 0
    for n, w in names:
        cols[n] = (c, c + w)
        c += w
    return cols, c


def _proj_weight(w_in, sample):
    splits = (ATTN_W, KV_W, KV_W, IDX_HEADS * IDX_DIM, IDX_DIM, IDX_HEADS, CONV_W, CONV_W, CONV_W, D_MODEL, D_MODEL)
    parts, c = [], 0
    for w in splits:
        parts.append(w_in[:, c:c + w])
        c += w
    wq, wk, wv, wiq, wik, wiw, wu, wgb, wgc, wga, wgbr = parts
    z = lambda n: jnp.zeros((D_MODEL, n), w_in.dtype)
    if sample:
        qparts = []
        for h in range(N_HEADS):
            g = h // REP
            qparts += [z(HEAD_DIM * g), wq[:, HEAD_DIM * h:HEAD_DIM * (h + 1)], z(KV_W - HEAD_DIM * (g + 1))]
        wq = jnp.concatenate(qparts, axis=1)
    slab = jnp.concatenate([wik, z(IW_LANE - IDX_DIM), wiw, z(LANES - IW_LANE - IDX_HEADS)], axis=1)
    return jnp.concatenate([wq, wk, wv, wiq, slab, wu, wgb, wgc, wga, wgbr], axis=1).astype(BF16)


def _proj(x3, mod3, mod_row0, norm_g, w_p, invf, conv_w, w_oc, state, *, sample, pos_base):
    G, R, _ = x3.shape
    n = G * R
    cols, wtot = _proj_layout(sample)
    if sample:
        bb = min(G, 64)
        tm = bb * R
        nb = G // bb
        xspec = pl.BlockSpec((bb, R, D_MODEL), lambda i: (i, 0, 0))
        shspec = pl.BlockSpec((bb, 1, D_MODEL), lambda i: (i, 0, 0))
        scspec = pl.BlockSpec((bb, 1, D_MODEL), lambda i: (i, 0, 1))
    else:
        tm = min(R, 512)
        nb = R // tm
        xspec = pl.BlockSpec((1, tm, D_MODEL), lambda i: (0, i, 0))
        shspec = pl.BlockSpec((1, 1, D_MODEL), lambda i: (mod_row0, 0, 0))
        scspec = pl.BlockSpec((1, 1, D_MODEL), lambda i: (mod_row0, 0, 1))
    row = lambda w: pl.BlockSpec((tm, w), lambda i: (i, 0))
    in_specs = [xspec, shspec, scspec, _resident((1, D_MODEL)), _resident((D_MODEL, wtot)), _resident((1, LANES)),
                _resident((CONV_K, CONV_W)), _resident((CONV_W, D_MODEL))]
    args = [x3, mod3, mod3, norm_g.reshape(1, D_MODEL), w_p, invf, conv_w, w_oc]
    if sample:
        in_specs.append(pl.BlockSpec((bb, CONV_K - 1, CONV_W), lambda i: (i, 0, 0)))
        args.append(state)
        out_specs = [pl.BlockSpec((bb, N_HEADS, R, KV_W), lambda i: (i, 0, 0, 0)),
                     pl.BlockSpec((bb, IDX_HEADS, R, IDX_DIM), lambda i: (i, 0, 0, 0)),
                     row(LANES), row(KV_W), row(KV_W), row(KV_W), row(KV_W), row(D_MODEL), row(D_MODEL),
                     pl.BlockSpec((bb, R, CONV_W), lambda i: (i, 0, 0))]
        out_shape = [jax.ShapeDtypeStruct((G, N_HEADS, R, KV_W), BF16),
                     jax.ShapeDtypeStruct((G, IDX_HEADS, R, IDX_DIM), BF16),
                     jax.ShapeDtypeStruct((n, LANES), F32),
                     jax.ShapeDtypeStruct((n, KV_W), F32), jax.ShapeDtypeStruct((n, KV_W), F32),
                     jax.ShapeDtypeStruct((n, KV_W), BF16), jax.ShapeDtypeStruct((n, KV_W), BF16),
                     jax.ShapeDtypeStruct((n, D_MODEL), F32), jax.ShapeDtypeStruct((n, D_MODEL), F32),
                     jax.ShapeDtypeStruct((G, R, CONV_W), F32)]
        scratch = []
    else:
        out_specs = [pl.BlockSpec((N_HEADS, tm, HEAD_DIM), lambda i: (0, i, 0)),
                     pl.BlockSpec((IDX_HEADS, tm, IDX_DIM), lambda i: (0, i, 0)),
                     row(LANES), row(KV_W), row(KV_W),
                     pl.BlockSpec((KV_W, tm), lambda i: (0, i)), pl.BlockSpec((IDX_DIM, tm), lambda i: (0, i)),
                     row(N_KV_HEADS * LANES), row(D_MODEL), row(D_MODEL),
                     pl.BlockSpec((SUBLANES, CONV_W), lambda i: (0, 0))]
        out_shape = [jax.ShapeDtypeStruct((N_HEADS, n, HEAD_DIM), BF16),
                     jax.ShapeDtypeStruct((IDX_HEADS, n, IDX_DIM), BF16),
                     jax.ShapeDtypeStruct((n, LANES), F32),
                     jax.ShapeDtypeStruct((n, KV_W), F32), jax.ShapeDtypeStruct((n, KV_W), F32),
                     jax.ShapeDtypeStruct((KV_W, n), BF16), jax.ShapeDtypeStruct((IDX_DIM, n), BF16),
                     jax.ShapeDtypeStruct((n, N_KV_HEADS * LANES), BF16),
                     jax.ShapeDtypeStruct((n, D_MODEL), F32), jax.ShapeDtypeStruct((n, D_MODEL), F32),
                     jax.ShapeDtypeStruct((SUBLANES, CONV_W), F32)]
        scratch = [pltpu.VMEM((tm + SUBLANES, CONV_W), F32)]
    return pl.pallas_call(
        functools.partial(_proj_kernel, sample=sample, tm=tm, pos_base=pos_base, cols=cols),
        grid=(nb,), in_specs=in_specs, out_specs=out_specs, out_shape=out_shape, scratch_shapes=scratch,
        compiler_params=_params("arbitrary"),
        name="proj_sample" if sample else "proj_prompt",
    )(*args)


def _lane_bcast(col):
    return jnp.broadcast_to(col, (col.shape[0], LANES))


def _lane_min(a):
    return jnp.min(a, axis=1, keepdims=True)


def _lane_max(a):
    return jnp.max(a, axis=1, keepdims=True)


def _lane_sum(a):
    return jnp.sum(a, axis=1, keepdims=True)


def _threshold_bracket(g_even, g_odd):
    lo = jnp.maximum(jnp.minimum(_lane_min(g_even), _lane_min(g_odd)), -BIG_SCORE)
    top = jnp.maximum(_lane_max(g_even), _lane_max(g_odd))
    return lo, top + jnp.maximum(jnp.abs(top) * 2.0 ** -10, 1e-30)


def _select_threshold(count_ge, max_below, lo, hi, done0, topk):
    live0 = jnp.logical_not(done0)

    def bisect(_, st):
        lo, hi, c_lo, c_hi = st
        act = live0 & (c_lo != topk)
        mid = 0.5 * lo + 0.5 * hi
        c = count_ge(mid)
        up = act & (c >= topk)
        dn = act & (c < topk)
        return jnp.where(up, mid, lo), jnp.where(dn, mid, hi), jnp.where(up, c, c_lo), jnp.where(dn, c, c_hi)

    lo, hi, c_lo, c_hi = lax.fori_loop(
        0, BISECT_STEPS, bisect, (lo, hi, jnp.full(lo.shape, INT_BIG, I32), jnp.zeros(lo.shape, I32)))

    def snap_cond(st):
        return jnp.max(st[4]) > 0

    def snap(st):
        lo, hi, c_lo, c_hi, open_ = st
        cand = max_below(hi)
        c = count_ge(cand)
        found = (open_ > 0) & (c >= topk)
        down = (open_ > 0) & (c < topk)
        return (jnp.where(found, cand, lo), jnp.where(down, cand, hi), jnp.where(found, c, c_lo),
                jnp.where(down, c, c_hi), down.astype(I32))

    lo, hi, c_lo, c_hi, _ = lax.while_loop(
        snap_cond, snap, (lo, hi, c_lo, c_hi, (live0 & (c_lo != topk)).astype(I32)))
    tie = live0 & (c_lo != topk)
    return jnp.where(done0, -BIG_SCORE, lo), tie, topk - c_hi


def _tie_cut(count_tie_below, need, r, nbits):
    def bit_step(b, x):
        cand = x + lax.shift_left(jnp.int32(1), nbits - 1 - b)
        return jnp.where(count_tie_below(cand) < r, cand, x)
    x = lax.fori_loop(0, nbits, bit_step, jnp.zeros(r.shape, I32))
    return jnp.where(need, x, INT_BIG)


def _selected(kk, kidx, thr_b, cut_b):
    return (kk > thr_b) | ((kk == thr_b) & (kidx <= cut_b))


def _attn_prompt_kernel(iq_ref, q_ref, iks_ref, ikt_ref, kt_ref, va_ref, o_ref,
                        score_scr, wb_scr, cut_scr, m_scr, qa_scr, acc_scr, kn_scr, *, tq, ck, topk, nbits):
    qi = pl.program_id(0)
    pair = CHUNKS_PER_STEP * ck
    n_pair = (qi * tq + tq + pair - 1) // pair
    nsub = ck // LANES

    w = iks_ref[:, IW_LANE:IW_LANE + IDX_HEADS] * (IDX_HEADS ** -0.5)
    for h in range(IDX_HEADS):
        wb_scr[h] = _lane_bcast(w[:, h:h + 1])
    qpos = qi * tq + lax.broadcasted_iota(I32, (tq, LANES), 0)
    lane = lax.broadcasted_iota(I32, (tq, LANES), 1)

    def for_chunks(fn, init):
        def body(c2, carry):
            for part in range(CHUNKS_PER_STEP):
                carry = fn(pl.multiple_of(c2 * pair + part * ck, ck), carry)
            return carry
        return lax.fori_loop(0, n_pair, body, init)

    def tile_of(k0):
        return lax.shift_right_logical(k0, LANES.bit_length() - 1)

    def score_chunk(k0, gmax):
        ikc = ikt_ref[:, pl.ds(k0, ck)]
        acc = jnp.zeros((tq, ck), F32)
        for h in range(IDX_HEADS):
            s = _dot(iq_ref[h], ikc)
            wbh = wb_scr[h]
            acc = acc + jnp.maximum(s, 0.0) * jnp.concatenate([wbh] * nsub, axis=1)
        gmax = list(gmax)
        for jj in range(nsub):
            kidx = k0 + jj * LANES + lane
            kj = jnp.where(kidx <= qpos, acc[:, jj * LANES:(jj + 1) * LANES], -jnp.inf)
            score_scr[tile_of(k0) + jj] = kj
            gmax[jj % 2] = jnp.maximum(gmax[jj % 2], kj)
        return tuple(gmax)
    floor = jnp.full((tq, LANES), -jnp.inf, F32)
    g_even, g_odd = for_chunks(score_chunk, (floor, floor))

    def count_ge(x_b):
        def body(k0, cnt):
            cnt = list(cnt)
            for jj in range(nsub):
                cnt[jj % 2] = cnt[jj % 2] + jnp.where(score_scr[tile_of(k0) + jj] >= x_b, 1, 0)
            return tuple(cnt)
        zero = jnp.zeros((tq, LANES), I32)
        c0, c1 = for_chunks(body, (zero, zero))
        return _lane_bcast(_lane_sum(c0 + c1))

    def max_below(x_b):
        def body(k0, m):
            for jj in range(nsub):
                kk = score_scr[tile_of(k0) + jj]
                m = jnp.maximum(m, jnp.where(kk < x_b, kk, -jnp.inf))
            return m
        return _lane_bcast(_lane_max(for_chunks(body, floor)))

    def count_tie_below(x_b):
        def body(k0, cnt):
            for jj in range(nsub):
                kk = score_scr[tile_of(k0) + jj]
                cnt = cnt + jnp.where((kk == thr_b) & (k0 + jj * LANES + lane < x_b), 1, 0)
            return cnt
        return _lane_bcast(_lane_sum(for_chunks(body, jnp.zeros((tq, LANES), I32))))

    lo0, hi0 = _threshold_bracket(g_even, g_odd)
    thr_b, tie, n_tie = _select_threshold(count_ge, max_below, _lane_bcast(lo0), _lane_bcast(hi0),
                                          qpos + 1 <= topk, topk)
    cut_scr[...] = jnp.full((tq, LANES), INT_BIG, I32)

    @pl.when(jnp.max(tie.astype(I32)) > 0)
    def _():
        cut_scr[...] = _tie_cut(count_tie_below, tie, n_tie, nbits)
    cut_b = cut_scr[...]

    def sel_tiles(k0):
        out = []
        for jj in range(nsub):
            kk = score_scr[tile_of(k0) + jj]
            sel = _selected(kk, k0 + jj * LANES + lane, thr_b, cut_b)
            out.append(jnp.concatenate([sel] * REP, axis=0))
        return out

    def group_q(g):
        return jnp.concatenate([q_ref[REP * g + r] for r in range(REP)], axis=0)

    kb = kt_ref[:, pl.ds(pl.multiple_of(qi * tq, tq), tq)].astype(F32)
    kn2 = jnp.max(jnp.sum((kb * kb).reshape(N_KV_HEADS, HEAD_DIM, tq), axis=1), axis=1, keepdims=True)

    @pl.when(qi == 0)
    def _():
        kn_scr[...] = jnp.zeros(kn_scr.shape, F32)
    kn2 = jnp.concatenate([kn2, jnp.zeros((SUBLANES - N_KV_HEADS, 1), F32)], axis=0)
    kn_scr[...] = jnp.maximum(kn_scr[...], jnp.broadcast_to(kn2, kn_scr.shape))

    lane2 = lax.broadcasted_iota(I32, (REP * tq, LANES), 1)
    ones_row = jnp.where(lax.broadcasted_iota(I32, (LANES - HEAD_DIM, ck), 0) == 0, 1.0, 0.0).astype(BF16)

    def set_shift(g, m_b):
        q2 = jnp.concatenate([group_q(g).astype(F32), jnp.zeros((REP * tq, LANES - HEAD_DIM), F32)], axis=1)
        qa_scr[g] = jnp.where(lane2 == HEAD_DIM, -m_b, q2).astype(BF16)

    def weighted_values():
        acc_scr[...] = jnp.zeros(acc_scr.shape, F32)

        def pv_chunk(k0, carry):
            sels = sel_tiles(k0)
            for g in range(N_KV_HEADS):
                k_aug = jnp.concatenate([kt_ref[HEAD_DIM * g:HEAD_DIM * (g + 1), pl.ds(k0, ck)], ones_row], axis=0)
                s = _dot(qa_scr[g], k_aug)
                ps = [jnp.where(sels[jj], jnp.exp2(s[:, jj * LANES:(jj + 1) * LANES]), 0.0).astype(BF16)
                      for jj in range(nsub)]
                acc_scr[g] += _dot(jnp.concatenate(ps, axis=1), va_ref[pl.ds(k0, ck), LANES * g:LANES * (g + 1)])
            return carry
        for_chunks(pv_chunk, 0)

    for g in range(N_KV_HEADS):
        qf = group_q(g).astype(F32)
        bound = jnp.sqrt(_lane_sum(qf * qf) * kn_scr[g:g + 1, 0:1])
        set_shift(g, _lane_bcast(bound * SHIFT_MARGIN + 1e-30))
    weighted_values()

    row_sum_ok = jnp.min(acc_scr[...][:, :, HEAD_DIM:HEAD_DIM + 1]) >= MIN_ROW_SUM

    @pl.when(jnp.logical_not(row_sum_ok))
    def _():
        m_scr[...] = jnp.full(m_scr.shape, -jnp.inf, F32)

        def max_chunk(k0, carry):
            sels = sel_tiles(k0)
            for g in range(N_KV_HEADS):
                s = _dot(group_q(g), kt_ref[HEAD_DIM * g:HEAD_DIM * (g + 1), pl.ds(k0, ck)])
                m = m_scr[g]
                for jj in range(nsub):
                    m = jnp.maximum(m, jnp.where(sels[jj], s[:, jj * LANES:(jj + 1) * LANES], -jnp.inf))
                m_scr[g] = m
            return carry
        for_chunks(max_chunk, 0)
        for g in range(N_KV_HEADS):
            set_shift(g, _lane_bcast(_lane_max(m_scr[g])))
        weighted_values()

    outs = []
    for g in range(N_KV_HEADS):
        acc = acc_scr[g]
        o = acc[:, :HEAD_DIM] / acc[:, HEAD_DIM:HEAD_DIM + 1]
        outs += [o[r * tq:(r + 1) * tq] for r in range(REP)]
    o_ref[...] = jnp.concatenate(outs, axis=1).astype(BF16)


def _attn_prompt(iq8, q8, iks, ikt, kt, va, topk):
    n = iks.shape[0]
    tq = min(128, n)
    ck = min(256, n // CHUNKS_PER_STEP)
    nbits = max(1, int(np.ceil(np.log2(n))))
    return pl.pallas_call(
        functools.partial(_attn_prompt_kernel, tq=tq, ck=ck, topk=topk, nbits=nbits),
        grid=(n // tq,),
        in_specs=[pl.BlockSpec((IDX_HEADS, tq, IDX_DIM), lambda i: (0, i, 0)),
                  pl.BlockSpec((N_HEADS, tq, HEAD_DIM), lambda i: (0, i, 0)),
                  pl.BlockSpec((tq, LANES), lambda i: (i, 0)),
                  _resident((IDX_DIM, n)), _resident((KV_W, n)), _resident((n, N_KV_HEADS * LANES))],
        out_specs=pl.BlockSpec((tq, ATTN_W), lambda i: (i, 0)),
        out_shape=jax.ShapeDtypeStruct((n, ATTN_W), BF16),
        scratch_shapes=[pltpu.VMEM((n // LANES, tq, LANES), F32),
                        pltpu.VMEM((IDX_HEADS, tq, LANES), F32),
                        pltpu.VMEM((tq, LANES), I32),
                        pltpu.VMEM((N_KV_HEADS, REP * tq, LANES), F32),
                        pltpu.VMEM((N_KV_HEADS, REP * tq, LANES), BF16),
                        pltpu.VMEM((N_KV_HEADS, REP * tq, LANES), F32),
                        pltpu.VMEM((SUBLANES, LANES), F32)],
        compiler_params=_params("arbitrary"),
        name="attn_prompt",
    )(iq8, q8, iks, ikt, kt, va)


def _attn_sample_kernel(pt_ref, iq_ref, q_ref, iks_ref, kbn_ref, vbn_ref, cik_hbm, ck_hbm, cv_hbm, o_ref,
                        ikp, kp, vp, sems, score_scr, cut_scr, *, nreq, n_pages, t_new, ppc, topk, nbits):
    b = pl.program_id(0)
    past = n_pages * PAGE_SIZE
    rows = IDX_HEADS * t_new
    qrows = nreq * t_new
    n_ck = n_pages // ppc
    ck = ppc * PAGE_SIZE
    reqs = range(nreq)

    def page_copy(src, dst, sem, slot, page):
        return pltpu.make_async_copy(src.at[page], dst.at[slot], sem)

    def start_pages(src, dst, sem):
        for r in reqs:
            def body(p, carry, r=r):
                page_copy(src, dst, sem, r * n_pages + p, pt_ref[b * nreq + r, p]).start()
                return carry
            lax.fori_loop(0, n_pages, body, 0)
    start_pages(cik_hbm, ikp, sems.at[0])
    start_pages(ck_hbm, kp, sems.at[1])
    start_pages(cv_hbm, vp, sems.at[2])

    def wait_pages(src, dst, sem):
        def body(slot, carry):
            page_copy(src, dst, sem, slot, 0).wait()
            return carry
        lax.fori_loop(0, nreq * n_pages, body, 0)

    def chunk_of(buf, r, c, width):
        return jnp.concatenate([buf[r * n_pages + c * ppc + j].reshape(width, PAGE_SIZE) for j in range(ppc)],
                               axis=1).astype(BF16)

    req_rows = lambda a, r: a[r * t_new:(r + 1) * t_new]
    iks = iks_ref[...]
    w = iks[:, IW_LANE:IW_LANE + IDX_HEADS] * (IDX_HEADS ** -0.5)
    wb = [_lane_bcast(jnp.concatenate([req_rows(w, r)[:, h:h + 1] for h in range(IDX_HEADS)], axis=0))
          for r in reqs]
    iq = [iq_ref[r].reshape(rows, IDX_DIM) for r in reqs]
    qrow = lax.broadcasted_iota(I32, (t_new, LANES), 0)
    lane = lax.broadcasted_iota(I32, (t_new, LANES), 1)
    lane_all = lax.broadcasted_iota(I32, (qrows, LANES), 1)

    def head_sum(t):
        acc = t[0:t_new]
        for h in range(1, IDX_HEADS):
            acc = acc + t[h * t_new:(h + 1) * t_new]
        return acc

    wait_pages(cik_hbm, ikp, sems.at[0])

    def score_chunk(c, gmax):
        gmax = list(gmax)
        for r in reqs:
            s = _dot(iq[r], chunk_of(ikp, r, c, IDX_DIM))
            for jj in range(ppc):
                kj = head_sum(jnp.maximum(s[:, jj * LANES:(jj + 1) * LANES], 0.0) * wb[r])
                score_scr[r * t_new:(r + 1) * t_new, pl.ds(pl.multiple_of(c * ck, ck) + jj * LANES, LANES)] = kj
                gmax[2 * r + jj % 2] = jnp.maximum(gmax[2 * r + jj % 2], kj)
        return tuple(gmax)
    floor = jnp.full((t_new, LANES), -jnp.inf, F32)
    gmax = lax.fori_loop(0, n_ck, score_chunk, (floor,) * (2 * nreq))

    pad_rows = lambda a: jnp.concatenate([a, jnp.zeros((LANES - t_new, a.shape[1]), a.dtype)], axis=0)
    new_ok = (lane <= qrow) & (lane < t_new)
    k_new = []
    for r in reqs:
        ik_r = req_rows(iks, r)[:, :IDX_DIM].astype(BF16)
        s_new = jnp.maximum(_dot_t(iq[r], pad_rows(ik_r)), 0.0) * wb[r]
        k_new.append(jnp.where(new_ok, head_sum(s_new), -jnp.inf))
    k_new_all = jnp.concatenate(k_new, axis=0)
    score_scr[:, past:past + LANES] = k_new_all

    n_tiles = past // LANES + 1
    tile_idx = lambda jt: jt * LANES + lane_all

    def count_over(pred):
        def run(cand):
            cand_b = _lane_bcast(cand)
            cnt = [jnp.zeros((qrows, LANES), I32)] * 2
            for jt in range(n_tiles):
                kk = score_scr[:, jt * LANES:(jt + 1) * LANES]
                cnt[jt % 2] = cnt[jt % 2] + jnp.where(pred(kk, tile_idx(jt), cand_b), 1, 0)
            return _lane_sum(cnt[0] + cnt[1])
        return run

    def max_below(x):
        x_b = _lane_bcast(x)
        m = jnp.full((qrows, LANES), -jnp.inf, F32)
        for jt in range(n_tiles):
            kk = score_scr[:, jt * LANES:(jt + 1) * LANES]
            m = jnp.maximum(m, jnp.where(kk < x_b, kk, -jnp.inf))
        return _lane_max(m)

    g_even = jnp.concatenate([jnp.maximum(gmax[2 * r], k_new[r]) for r in reqs], axis=0)
    g_odd = jnp.concatenate([gmax[2 * r + 1] for r in reqs], axis=0)
    lo0, hi0 = _threshold_bracket(g_even, g_odd)
    done0 = jnp.zeros((qrows, 1), jnp.bool_)
    thr, tie, n_tie = _select_threshold(count_over(lambda kk, kidx, cb: kk >= cb), max_below, lo0, hi0, done0, topk)
    thr_b = _lane_bcast(thr)
    cut_scr[...] = jnp.full((qrows, LANES), INT_BIG, I32)

    @pl.when(jnp.max(tie.astype(I32)) > 0)
    def _():
        cut = _tie_cut(count_over(lambda kk, kidx, cb: (kk == thr_b) & (kidx < cb)), tie, n_tie, nbits)
        cut_scr[...] = _lane_bcast(cut)
    cut_b = cut_scr[...]

    def sel_rows(r, kk, kidx):
        sel = _selected(kk, kidx, req_rows(thr_b, r), req_rows(cut_b, r))
        return jnp.concatenate([sel] * N_HEADS, axis=0)

    def sel_chunk(r, c, jj):
        kk = score_scr[r * t_new:(r + 1) * t_new, pl.ds(pl.multiple_of(c * ck, ck) + jj * LANES, LANES)]
        return sel_rows(r, kk, c * ck + jj * LANES + lane)

    wait_pages(ck_hbm, kp, sems.at[1])
    wait_pages(cv_hbm, vp, sems.at[2])
    q = [q_ref[r].reshape(rows, KV_W) for r in reqs]
    s_new = [_dot_t(q[r], pad_rows(req_rows(kbn_ref[...], r))) for r in reqs]
    sel_new = [sel_rows(r, k_new[r], past + lane) for r in reqs]

    def max_chunk(c, ms):
        ms = list(ms)
        for r in reqs:
            s = _dot(q[r], chunk_of(kp, r, c, KV_W))
            for jj in range(ppc):
                ms[r] = jnp.maximum(ms[r], jnp.where(sel_chunk(r, c, jj), s[:, jj * LANES:(jj + 1) * LANES], -jnp.inf))
        return tuple(ms)
    ms = lax.fori_loop(0, n_ck, max_chunk, tuple(jnp.where(sel_new[r], s_new[r], -jnp.inf) for r in reqs))
    m_b = [_lane_bcast(_lane_max(ms[r])) for r in reqs]

    p_new = [jnp.where(sel_new[r], jnp.exp2(s_new[r] - m_b[r]), 0.0) for r in reqs]
    acc0 = [_dot(p_new[r].astype(BF16), pad_rows(req_rows(vbn_ref[...], r))) for r in reqs]

    def pv_chunk(c, carry):
        ls, accs = list(carry[0]), list(carry[1])
        for r in reqs:
            s = _dot(q[r], chunk_of(kp, r, c, KV_W))
            ps = []
            for jj in range(ppc):
                p = jnp.where(sel_chunk(r, c, jj), jnp.exp2(s[:, jj * LANES:(jj + 1) * LANES] - m_b[r]), 0.0)
                ls[r] = ls[r] + p
                ps.append(p.astype(BF16))
            accs[r] = accs[r] + _dot_t(jnp.concatenate(ps, axis=1), chunk_of(vp, r, c, KV_W))
        return tuple(ls), tuple(accs)
    ls, accs = lax.fori_loop(0, n_ck, pv_chunk, (tuple(p_new), tuple(acc0)))
    for r in reqs:
        o = accs[r] / _lane_sum(ls[r])
        o_ref[r] = o.reshape(N_HEADS, t_new, KV_W).astype(BF16)


def _attn_sample(page_table, iq_s, q_s, iks_s, kb_s, vb_s, cache_ikt, cache_kt, cache_vt, topk):
    db, n_pages = page_table.shape
    t_new = iq_s.shape[2]
    past = n_pages * PAGE_SIZE
    assert past >= 2 * LANES and past + 1 > topk, "the threshold bracket needs two full key tiles"
    ppc = 64 if n_pages % 64 == 0 else 2
    nreq = 2 if db % 2 == 0 else 1
    nbits = max(1, int(np.ceil(np.log2(past + LANES))))
    grid_spec = pltpu.PrefetchScalarGridSpec(
        num_scalar_prefetch=1,
        grid=(db // nreq,),
        in_specs=[pl.BlockSpec((nreq, IDX_HEADS, t_new, IDX_DIM), lambda b, pt: (b, 0, 0, 0)),
                  pl.BlockSpec((nreq, N_HEADS, t_new, KV_W), lambda b, pt: (b, 0, 0, 0)),
                  pl.BlockSpec((nreq * t_new, LANES), lambda b, pt: (b, 0)),
                  pl.BlockSpec((nreq * t_new, KV_W), lambda b, pt: (b, 0)),
                  pl.BlockSpec((nreq * t_new, KV_W), lambda b, pt: (b, 0)),
                  pl.BlockSpec(memory_space=pl.ANY), pl.BlockSpec(memory_space=pl.ANY),
                  pl.BlockSpec(memory_space=pl.ANY)],
        out_specs=pl.BlockSpec((nreq, N_HEADS, t_new, KV_W), lambda b, pt: (b, 0, 0, 0)),
        scratch_shapes=[pltpu.VMEM((nreq * n_pages, IDX_DIM, PAGE_SIZE), F32),
                        pltpu.VMEM((nreq * n_pages, N_KV_HEADS, HEAD_DIM, PAGE_SIZE), F32),
                        pltpu.VMEM((nreq * n_pages, N_KV_HEADS, HEAD_DIM, PAGE_SIZE), F32),
                        pltpu.SemaphoreType.DMA((3,)),
                        pltpu.VMEM((nreq * t_new, past + LANES), F32),
                        pltpu.VMEM((nreq * t_new, LANES), I32)])
    return pl.pallas_call(
        functools.partial(_attn_sample_kernel, nreq=nreq, n_pages=n_pages, t_new=t_new, ppc=ppc, topk=topk,
                          nbits=nbits),
        grid_spec=grid_spec,
        out_shape=jax.ShapeDtypeStruct((db, N_HEADS, t_new, KV_W), BF16),
        compiler_params=_params("arbitrary"),
        name="attn_sample",
    )(page_table, iq_s, q_s, iks_s, kb_s, vb_s, cache_ikt, cache_kt, cache_vt)


def _mix_kernel(attn_ref, sga_ref, yc_ref, x_ref, g1_ref, sh2_ref, sc2_ref, n2_ref, woa_ref, wout_ref,
                wr_hi_ref, wr_lo_ref, br_ref, x1_ref, h2_ref, comb_ref, *, sample, tm):
    if sample:
        a = attn_ref[...]
        y_attn = jnp.zeros((tm, D_MODEL), F32)
        for h in range(N_HEADS):
            y_attn = y_attn + _dot(a[:, h].reshape(tm, KV_W), woa_ref[h])
    else:
        y_attn = _dot(attn_ref[...], woa_ref[...])
    mixed = _dot((sga_ref[...] * y_attn + yc_ref[...]).astype(BF16), wout_ref[...])
    x3 = x_ref[...]
    g, r, _ = x3.shape
    x1 = x3 + g1_ref[...] * mixed.reshape(g, r, D_MODEL)
    x1_ref[...] = x1
    h2 = _modnorm(x1, n2_ref[...], sc2_ref[...], sh2_ref[...])
    h2_ref[...] = h2.astype(BF16)

    lg = _dot3(h2, wr_hi_ref[...], wr_lo_ref[...]) + br_ref[...]
    lane = lax.broadcasted_iota(I32, (tm, LANES), 1)
    is_g = lane < N_GROUPS
    is_e = (lane >= ROUTER_E0) & (lane < ROUTER_E0 + N_EXPERTS)
    first = lambda msk: _lane_min(jnp.where(msk, lane, LANES))
    gl = jnp.where(is_g, lg, -jnp.inf)
    gmax = _lane_max(gl)
    g_sel = first(gl == gmax)
    g_w = 1.0 / _lane_sum(jnp.where(is_g, jnp.exp(lg - gmax), 0.0))
    in_grp = is_e & (((lane - ROUTER_E0) >> 2) == g_sel)
    me = jnp.where(in_grp, lg, -jnp.inf)
    v1 = _lane_max(me)
    i1 = first(in_grp & (me == v1))
    rest = in_grp & (lane != i1)
    me2 = jnp.where(rest, lg, -jnp.inf)
    v2 = _lane_max(me2)
    i2 = first(rest & (me2 == v2))
    e2 = jnp.exp(v2 - v1)
    den = 1.0 + e2
    comb_ref[...] = jnp.where(lane == i1, g_w / den, 0.0) + jnp.where(lane == i2, g_w * e2 / den, 0.0)


def _mix(attn, sga, yc, x3, mod3, mod_row0, norm2_g, w_oa, w_out, wr_hi, wr_lo, b_r, *, sample):
    G, R, _ = x3.shape
    n = G * R
    if sample:
        bb = min(G, 32)
        tm = bb * R
        nb = G // bb
        xspec = pl.BlockSpec((bb, R, D_MODEL), lambda i: (i, 0, 0))
        mspec = lambda c: pl.BlockSpec((bb, 1, D_MODEL), lambda i: (i, 0, c))
        aspec = pl.BlockSpec((bb, N_HEADS, R, KV_W), lambda i: (i, 0, 0, 0))
        woa_spec = _resident((N_HEADS, KV_W, D_MODEL))
    else:
        tm = min(R, 256)
        nb = R // tm
        xspec = pl.BlockSpec((1, tm, D_MODEL), lambda i: (0, i, 0))
        mspec = lambda c: pl.BlockSpec((1, 1, D_MODEL), lambda i: (mod_row0, 0, c))
        aspec = pl.BlockSpec((tm, ATTN_W), lambda i: (i, 0))
        woa_spec = _resident((ATTN_W, D_MODEL))
    row = lambda w: pl.BlockSpec((tm, w), lambda i: (i, 0))
    return pl.pallas_call(
        functools.partial(_mix_kernel, sample=sample, tm=tm),
        grid=(nb,),
        in_specs=[aspec, row(D_MODEL), row(D_MODEL), xspec, mspec(2), mspec(3), mspec(4), _resident((1, D_MODEL)),
                  woa_spec, _resident((D_MODEL, D_MODEL)), _resident((D_MODEL, LANES)), _resident((D_MODEL, LANES)),
                  _resident((1, LANES))],
        out_specs=[xspec, row(D_MODEL), row(LANES)],
        out_shape=[jax.ShapeDtypeStruct((G, R, D_MODEL), F32), jax.ShapeDtypeStruct((n, D_MODEL), BF16),
                   jax.ShapeDtypeStruct((n, LANES), F32)],
        compiler_params=_params("arbitrary"),
        name="mix_sample" if sample else "mix_prompt",
    )(attn, sga, yc, x3, mod3, mod3, mod3, norm2_g.reshape(1, D_MODEL), w_oa, w_out, wr_hi, wr_lo, b_r)


def _moe_kernel(h2_ref, comb_ref, x1_ref, g2_ref, fg_ref, weg_ref, weu_ref, wed_ref, y_ref, acc_scr, *, tm, epb):
    j = pl.program_id(1)

    @pl.when(j == 0)
    def _():
        acc_scr[...] = jnp.zeros(acc_scr.shape, F32)

    h2 = h2_ref[...]
    a = _dot(h2, weg_ref[...])
    u = _dot(h2, weu_ref[...])
    hid = a * _sigmoid(a) * u
    comb = comb_ref[...]
    lane = lax.broadcasted_iota(I32, (tm, LANES), 1)
    parts = []
    for e in range(epb):
        ce = _lane_sum(jnp.where(lane == ROUTER_E0 + j * epb + e, comb, 0.0))
        parts.append((hid[:, e * EXPERT_FF:(e + 1) * EXPERT_FF] * ce).astype(BF16))
    acc_scr[...] += _dot(jnp.concatenate(parts, axis=1), wed_ref[...])

    @pl.when(j == pl.num_programs(1) - 1)
    def _():
        x1 = x1_ref[...]
        g, r, _ = x1.shape
        x2 = x1 + g2_ref[...] * acc_scr[...].reshape(g, r, D_MODEL)
        ms = jnp.mean(x2 * x2, axis=-1, keepdims=True)
        y_ref[...] = x2 * lax.rsqrt(ms + EPS) * fg_ref[...][None]


def _moe(h2, comb, x1, mod3, mod_row0, final_g, w_eg, w_eu, w_ed, *, sample):
    G, R, _ = x1.shape
    epb = EXP_PER_GROUP
    nj = N_EXPERTS // epb
    wblk = epb * EXPERT_FF
    if sample:
        bb = min(G, 64)
        tm = bb * R
        nb = G // bb
        xspec = pl.BlockSpec((bb, R, D_MODEL), lambda i, j: (i, 0, 0))
        g2spec = pl.BlockSpec((bb, 1, D_MODEL), lambda i, j: (i, 0, 5))
    else:
        tm = min(R, 512)
        nb = R // tm
        xspec = pl.BlockSpec((1, tm, D_MODEL), lambda i, j: (0, i, 0))
        g2spec = pl.BlockSpec((1, 1, D_MODEL), lambda i, j: (mod_row0, 0, 5))
    return pl.pallas_call(
        functools.partial(_moe_kernel, tm=tm, epb=epb),
        grid=(nb, nj),
        in_specs=[pl.BlockSpec((tm, D_MODEL), lambda i, j: (i, 0)),
                  pl.BlockSpec((tm, LANES), lambda i, j: (i, 0)),
                  xspec, g2spec,
                  pl.BlockSpec((1, D_MODEL), lambda i, j: (0, 0)),
                  pl.BlockSpec((D_MODEL, wblk), lambda i, j: (0, j)),
                  pl.BlockSpec((D_MODEL, wblk), lambda i, j: (0, j)),
                  pl.BlockSpec((wblk, D_MODEL), lambda i, j: (j, 0))],
        out_specs=xspec,
        out_shape=jax.ShapeDtypeStruct((G, R, D_MODEL), F32),
        scratch_shapes=[pltpu.VMEM((tm, D_MODEL), F32)],
        compiler_params=_params("arbitrary", "arbitrary"),
        name="moe_sample" if sample else "moe_prompt",
    )(h2, comb, x1, mod3, final_g.reshape(1, D_MODEL), w_eg, w_eu, w_ed)


def kernel(x_prompt, x_sample, cache_k, cache_v, cache_idx_k, state_conv, page_table, c_prompt, c_sample,
           w_ada, b_ada, norm1_g, norm2_g, w_in_mix, conv_w, w_o_attn, w_o_conv, w_out,
           w_router_group, b_router_group, w_router_expert, b_router_expert,
           w_exp_gate, w_exp_up, w_exp_down, final_norm_g):
    depth = w_ada.shape[0]
    assert depth == 1, "single-layer step"
    B, S, _ = x_prompt.shape
    assert B == 1, "one prompt sequence"
    DB, T, _ = x_sample.shape
    assert T == SUBLANES, "sample tokens per request must fill one sublane tile"
    n_pages = page_table.shape[1]
    past = n_pages * PAGE_SIZE

    n_rows = DB + B
    pad = (-n_rows) % SUBLANES
    c_all = jnp.concatenate([c_sample, c_prompt, jnp.zeros((pad, D_MODEL), F32)], axis=0)
    mod = _ada(c_all, w_ada[0], b_ada[0])
    mod3 = mod.reshape(n_rows + pad, 1, 6 * D_MODEL)

    half = HEAD_DIM // 8
    inv_freq = ROPE_THETA ** (-jnp.arange(half, dtype=F32) / half)
    lane = np.arange(LANES)
    invf = jnp.where((lane % HEAD_DIM) < 2 * half, inv_freq[lane % half], 0.0).reshape(1, LANES).astype(F32)
    w_in = w_in_mix[0]
    w_oc = w_o_conv[0].astype(BF16)
    w_oa = w_o_attn[0]
    w_oa_p = w_oa.astype(BF16)
    w_oa_s = jnp.stack([
        jnp.concatenate([jnp.zeros((HEAD_DIM * (h // REP), D_MODEL), F32), w_oa[HEAD_DIM * h:HEAD_DIM * (h + 1)],
                         jnp.zeros((KV_W - HEAD_DIM * (h // REP + 1), D_MODEL), F32)], axis=0)
        for h in range(N_HEADS)]).astype(BF16)
    w_o = w_out[0].astype(BF16)
    zr = jnp.zeros((D_MODEL, LANES - ROUTER_E0 - N_EXPERTS), F32)
    w_r = jnp.concatenate([w_router_group[0], w_router_expert[0], zr], axis=1)
    wr_hi = w_r.astype(BF16)
    wr_lo = (w_r - wr_hi.astype(F32)).astype(BF16)
    b_r = jnp.concatenate([b_router_group[0], b_router_expert[0],
                           jnp.zeros((LANES - ROUTER_E0 - N_EXPERTS,), F32)]).reshape(1, LANES)
    cat_e = lambda w: jnp.transpose(w, (1, 0, 2)).reshape(D_MODEL, N_EXPERTS * EXPERT_FF).astype(BF16)
    w_eg = cat_e(w_exp_gate[0])
    w_eu = cat_e(w_exp_up[0])
    w_ed = w_exp_down[0].reshape(N_EXPERTS * EXPERT_FF, D_MODEL).astype(BF16)

    xp3 = x_prompt
    (q8, iq8, iks_p, k_p, v_p, kt, ikt, va_p, yc_p, sga_p, cst_p) = _proj(
        xp3, mod3, DB, norm1_g[0], _proj_weight(w_in, False), invf, conv_w[0], w_oc, None, sample=False, pos_base=0)
    attn_p = _attn_prompt(iq8, q8, iks_p, ikt, kt, va_p, min(TOPK_MAX, S // 4))
    x1_p, h2_p, comb_p = _mix(attn_p, sga_p, yc_p, xp3, mod3, DB, norm2_g[0], w_oa_p, w_o, wr_hi, wr_lo, b_r,
                              sample=False)
    y_p = _moe(h2_p, comb_p, x1_p, mod3, DB, final_norm_g, w_eg, w_eu, w_ed, sample=False)

    cache_ikt = jnp.transpose(cache_idx_k[0], (0, 2, 1))
    cache_kt = jnp.transpose(cache_k[0], (0, 2, 3, 1))
    cache_vt = jnp.transpose(cache_v[0], (0, 2, 3, 1))
    (q_s, iq_s, iks_s, k_s, v_s, kb_s, vb_s, yc_s, sga_s, cst_s) = _proj(
        x_sample, mod3, 0, norm1_g[0], _proj_weight(w_in, True), invf, conv_w[0], w_oc, state_conv[0],
        sample=True, pos_base=past)
    attn_s = _attn_sample(page_table, iq_s, q_s, iks_s, kb_s, vb_s, cache_ikt, cache_kt, cache_vt,
                          min(TOPK_MAX, (past + T) // 4))
    x1_s, h2_s, comb_s = _mix(attn_s, sga_s, yc_s, x_sample, mod3, 0, norm2_g[0], w_oa_s, w_o, wr_hi, wr_lo, b_r,
                              sample=True)
    y_s = _moe(h2_s, comb_s, x1_s, mod3, 0, final_norm_g, w_eg, w_eu, w_ed, sample=True)

    return (y_p, y_s,
            k_p.reshape(1, B, S, N_KV_HEADS, HEAD_DIM), v_p.reshape(1, B, S, N_KV_HEADS, HEAD_DIM),
            iks_p[:, :IDX_DIM].reshape(1, B, S, IDX_DIM),
            cst_p[SUBLANES - (CONV_K - 1):].reshape(1, B, CONV_K - 1, CONV_W),
            k_s.reshape(1, DB, T, N_KV_HEADS, HEAD_DIM), v_s.reshape(1, DB, T, N_KV_HEADS, HEAD_DIM),
            iks_s[:, :IDX_DIM].reshape(1, DB, T, IDX_DIM),
            cst_s[:, T - (CONV_K - 1):].reshape(1, DB, CONV_K - 1, CONV_W))
```

```python
import functools

import jax
import jax.numpy as jnp
import numpy as np
from jax import lax
from jax.experimental import pallas as pl
from jax.experimental.pallas import tpu as pltpu

F32 = jnp.float32
BF16 = jnp.bfloat16
I32 = jnp.int32

D_MODEL = 1024
N_HEADS = 8
N_KV_HEADS = 4
HEAD_DIM = 64
ATTN_W = N_HEADS * HEAD_DIM
KV_W = N_KV_HEADS * HEAD_DIM
ROPE_THETA = 500000.0
IDX_HEADS = 8
IDX_DIM = 64
TOPK_MAX = 256
CONV_W = 512
CONV_K = 3
N_GROUPS = 4
EXP_PER_GROUP = 4
N_EXPERTS = N_GROUPS * EXP_PER_GROUP
EXPERT_FF = 256
EPS = 1e-6
PAGE_SIZE = 128

LANES = 128
SUBLANES = 8
VMEM_LIMIT = 56 * 1024 * 1024
INT_BIG = 2 ** 30
BIG_SCORE = 3.0e38
SHIFT_MARGIN = 1.0 + 2.0 ** -6
MIN_ROW_SUM = 2.0 ** -60
CHUNKS_PER_STEP = 4
BISECT_STEPS = 13
LOG2E = 1.4426950408889634
Q_SCALE = HEAD_DIM ** -0.5 * LOG2E
IW_LANE = 112
ROUTER_E0 = 4
REP = N_HEADS // N_KV_HEADS


def _dot(a, b):
    return jnp.dot(a, b, preferred_element_type=F32)


def _dot_t(a, b):
    return lax.dot_general(a, b, (((1,), (1,)), ((), ())), preferred_element_type=F32)


def _split_bf16(a):
    hi = a.astype(BF16)
    lo = (a - hi.astype(F32)).astype(BF16)
    return hi, lo


def _dot3(a, w_hi, w_lo):
    a_hi, a_lo = _split_bf16(a)
    return _dot(a_hi, w_hi) + (_dot(a_hi, w_lo) + _dot(a_lo, w_hi))


def _sigmoid(x):
    return 1.0 / (1.0 + jnp.exp(-x))


def _params(*sem):
    return pltpu.CompilerParams(dimension_semantics=sem, vmem_limit_bytes=VMEM_LIMIT)


def _resident(shape):
    nd = len(shape)
    return pl.BlockSpec(shape, lambda *_: (0,) * nd, pipeline_mode=pl.Buffered(1))


def _ada_kernel(c_ref, w_ref, b_ref, o_ref):
    c = c_ref[...]
    a = c * _sigmoid(c)
    w = w_ref[...]
    w_hi, w_lo = _split_bf16(w)
    o_ref[...] = _dot3(a, w_hi, w_lo) + b_ref[...]


def _ada(c_all, w_ada, b_ada):
    rows = c_all.shape[0]
    n = w_ada.shape[1]
    tn = 512
    return pl.pallas_call(
        _ada_kernel,
        grid=(n // tn,),
        in_specs=[pl.BlockSpec((rows, D_MODEL), lambda j: (0, 0)),
                  pl.BlockSpec((D_MODEL, tn), lambda j: (0, j)),
                  pl.BlockSpec((1, tn), lambda j: (0, j))],
        out_specs=pl.BlockSpec((rows, tn), lambda j: (0, j)),
        out_shape=jax.ShapeDtypeStruct((rows, n), F32),
        compiler_params=_params("arbitrary"),
        name="ada",
    )(c_all, w_ada, b_ada.reshape(1, n))


def _modnorm(x3, g, sc3, sh3):
    ms = jnp.mean(x3 * x3, axis=-1, keepdims=True)
    xn = x3 * lax.rsqrt(ms + EPS)
    h = xn * g[None] * (1.0 + sc3) + sh3
    return h.reshape(x3.shape[0] * x3.shape[1], x3.shape[2])


def _proj_kernel(*refs, sample, tm, pos_base, cols):
    if sample:
        (x_ref, sh_ref, sc_ref, g_ref, w_ref, invf_ref, cw_ref, woc_ref, st_ref,
         qo_ref, iqo_ref, iks_ref, k_ref, v_ref, kb_ref, vb_ref, yc_ref, sga_ref, cst_ref) = refs
    else:
        (x_ref, sh_ref, sc_ref, g_ref, w_ref, invf_ref, cw_ref, woc_ref,
         qo_ref, iqo_ref, iks_ref, k_ref, v_ref, kt_ref, ikt_ref, vb_ref, yc_ref, sga_ref, cst_ref,
         cu_scr) = refs
    i = pl.program_id(0)
    hb = _modnorm(x_ref[...], g_ref[...], sc_ref[...], sh_ref[...]).astype(BF16)

    def proj(name):
        c0, c1 = cols[name]
        return _dot(hb, w_ref[:, c0:c1])

    lane = lax.broadcasted_iota(I32, (tm, LANES), 1)
    row = lax.broadcasted_iota(I32, (tm, LANES), 0)
    if sample:
        pos = pos_base + (row & (SUBLANES - 1))
    else:
        pos = i * tm + row
    ang = pos.astype(F32) * invf_ref[...]
    cosv = jnp.cos(ang)
    sinv = jnp.sin(ang)
    j = lane & (HEAD_DIM - 1)
    half = HEAD_DIM // 8
    rc = jnp.where(j < 2 * half, cosv, 1.0)
    rs_lo = jnp.where((j >= half) & (j < 2 * half), sinv, 0.0)
    rs_hi = jnp.where(j < half, -sinv, 0.0)

    def rope(xs):
        return xs * rc + pltpu.roll(xs, half, 1) * rs_lo + pltpu.roll(xs, LANES - half, 1) * rs_hi

    pq = proj("q")
    if sample:
        for h in range(N_HEADS):
            blk = pq[:, KV_W * h:KV_W * (h + 1)]
            r = jnp.concatenate([rope(blk[:, :LANES]), rope(blk[:, LANES:])], axis=1) * Q_SCALE
            qo_ref[:, h] = r.reshape(tm // SUBLANES, SUBLANES, KV_W).astype(BF16)
    else:
        for s in range(ATTN_W // LANES):
            r = (rope(pq[:, LANES * s:LANES * (s + 1)]) * Q_SCALE).astype(BF16)
            qo_ref[2 * s] = r[:, :HEAD_DIM]
            qo_ref[2 * s + 1] = r[:, HEAD_DIM:]

    pk = proj("k")
    for s in range(KV_W // LANES):
        rk = rope(pk[:, LANES * s:LANES * (s + 1)])
        k_ref[:, LANES * s:LANES * (s + 1)] = rk
        if sample:
            kb_ref[:, LANES * s:LANES * (s + 1)] = rk.astype(BF16)
        else:
            kt_ref[LANES * s:LANES * (s + 1), :] = rk.T.astype(BF16)

    pv = proj("v")
    v_ref[...] = pv
    if sample:
        vb_ref[...] = pv.astype(BF16)
    else:
        tail = jnp.where(lane == HEAD_DIM, 1.0, 0.0)
        for s in range(KV_W // LANES):
            slab = pv[:, LANES * s:LANES * (s + 1)]
            vb_ref[:, LANES * 2 * s:LANES * (2 * s + 1)] = jnp.where(lane < HEAD_DIM, slab, tail).astype(BF16)
            vb_ref[:, LANES * (2 * s + 1):LANES * (2 * s + 2)] = jnp.where(
                lane < HEAD_DIM, pltpu.roll(slab, HEAD_DIM, 1), tail).astype(BF16)

    piq = proj("iq")
    for s in range(IDX_HEADS * IDX_DIM // LANES):
        r = (rope(piq[:, LANES * s:LANES * (s + 1)]) * (IDX_DIM ** -0.5)).astype(BF16)
        if sample:
            iqo_ref[:, 2 * s] = r[:, :IDX_DIM].reshape(tm // SUBLANES, SUBLANES, IDX_DIM)
            iqo_ref[:, 2 * s + 1] = r[:, IDX_DIM:].reshape(tm // SUBLANES, SUBLANES, IDX_DIM)
        else:
            iqo_ref[2 * s] = r[:, :IDX_DIM]
            iqo_ref[2 * s + 1] = r[:, IDX_DIM:]

    slab = rope(proj("iks"))
    iks_ref[...] = slab
    if not sample:
        ikt_ref[...] = slab.T[:IDX_DIM, :].astype(BF16)

    cu = proj("gc") * proj("u")
    cw = cw_ref[...]
    if sample:
        g8 = tm // SUBLANES
        st = st_ref[...]
        grp = (g8, SUBLANES, CONV_W)
        s0 = jnp.broadcast_to(st[:, 0:1, :], grp).reshape(tm, CONV_W)
        s1 = jnp.broadcast_to(st[:, 1:2, :], grp).reshape(tm, CONV_W)
        t = lax.broadcasted_iota(I32, (tm, CONV_W), 0) & (SUBLANES - 1)
        m1 = jnp.where(t >= 1, pltpu.roll(cu, 1, 0), s1)
        m2 = jnp.where(t >= 2, pltpu.roll(cu, 2, 0), jnp.where(t == 1, s1, s0))
        conv = m2 * cw[0:1] + m1 * cw[1:2] + cu * cw[2:3]
        cst_ref[...] = cu.reshape(grp)
    else:
        @pl.when(i == 0)
        def _():
            cu_scr[0:SUBLANES, :] = jnp.zeros((SUBLANES, CONV_W), F32)
        cu_scr[SUBLANES:SUBLANES + tm, :] = cu
        m1 = cu_scr[SUBLANES - 1:SUBLANES - 1 + tm, :]
        m2 = cu_scr[SUBLANES - 2:SUBLANES - 2 + tm, :]
        conv = m2 * cw[0:1] + m1 * cw[1:2] + cu * cw[2:3]
        tail = cu_scr[tm:tm + SUBLANES, :]
        cu_scr[0:SUBLANES, :] = tail
        cst_ref[...] = tail
    zc = (proj("gb_conv") * conv).astype(BF16)
    yconv = _dot(zc, woc_ref[...])
    sga_ref[...] = _sigmoid(proj("ga"))
    yc_ref[...] = _sigmoid(proj("gb")) * yconv


def _proj_layout(sample):
    names = [("q", N_HEADS * KV_W if sample else ATTN_W), ("k", KV_W), ("v", KV_W),
             ("iq", IDX_HEADS * IDX_DIM), ("iks", LANES), ("u", CONV_W), ("gb_conv", CONV_W), ("gc", CONV_W),
             ("ga", D_MODEL), ("gb", D_MODEL)]
    cols, c = {}, 0
    for n, w in names:
        cols[n] = (c, c + w)
        c += w
    return cols, c


def _proj_weight(w_in, sample):
    splits = (ATTN_W, KV_W, KV_W, IDX_HEADS * IDX_DIM, IDX_DIM, IDX_HEADS, CONV_W, CONV_W, CONV_W, D_MODEL, D_MODEL)
    parts, c = [], 0
    for w in splits:
        parts.append(w_in[:, c:c + w])
        c += w
    wq, wk, wv, wiq, wik, wiw, wu, wgb, wgc, wga, wgbr = parts
    z = lambda n: jnp.zeros((D_MODEL, n), w_in.dtype)
    if sample:
        qparts = []
        for h in range(N_HEADS):
            g = h // REP
            qparts += [z(HEAD_DIM * g), wq[:, HEAD_DIM * h:HEAD_DIM * (h + 1)], z(KV_W - HEAD_DIM * (g + 1))]
        wq = jnp.concatenate(qparts, axis=1)
    slab = jnp.concatenate([wik, z(IW_LANE - IDX_DIM), wiw, z(LANES - IW_LANE - IDX_HEADS)], axis=1)
    return jnp.concatenate([wq, wk, wv, wiq, slab, wu, wgb, wgc, wga, wgbr], axis=1).astype(BF16)


def _proj(x3, mod3, mod_row0, norm_g, w_p, invf, conv_w, w_oc, state, *, sample, pos_base):
    G, R, _ = x3.shape
    n = G * R
    cols, wtot = _proj_layout(sample)
    if sample:
        bb = min(G, 64)
        tm = bb * R
        nb = G // bb
        xspec = pl.BlockSpec((bb, R, D_MODEL), lambda i: (i, 0, 0))
        shspec = pl.BlockSpec((bb, 1, D_MODEL), lambda i: (i, 0, 0))
        scspec = pl.BlockSpec((bb, 1, D_MODEL), lambda i: (i, 0, 1))
    else:
        tm = min(R, 512)
        nb = R // tm
        xspec = pl.BlockSpec((1, tm, D_MODEL), lambda i: (0, i, 0))
        shspec = pl.BlockSpec((1, 1, D_MODEL), lambda i: (mod_row0, 0, 0))
        scspec = pl.BlockSpec((1, 1, D_MODEL), lambda i: (mod_row0, 0, 1))
    row = lambda w: pl.BlockSpec((tm, w), lambda i: (i, 0))
    in_specs = [xspec, shspec, scspec, _resident((1, D_MODEL)), _resident((D_MODEL, wtot)), _resident((1, LANES)),
                _resident((CONV_K, CONV_W)), _resident((CONV_W, D_MODEL))]
    args = [x3, mod3, mod3, norm_g.reshape(1, D_MODEL), w_p, invf, conv_w, w_oc]
    if sample:
        in_specs.append(pl.BlockSpec((bb, CONV_K - 1, CONV_W), lambda i: (i, 0, 0)))
        args.append(state)
        out_specs = [pl.BlockSpec((bb, N_HEADS, R, KV_W), lambda i: (i, 0, 0, 0)),
                     pl.BlockSpec((bb, IDX_HEADS, R, IDX_DIM), lambda i: (i, 0, 0, 0)),
                     row(LANES), row(KV_W), row(KV_W), row(KV_W), row(KV_W), row(D_MODEL), row(D_MODEL),
                     pl.BlockSpec((bb, R, CONV_W), lambda i: (i, 0, 0))]
        out_shape = [jax.ShapeDtypeStruct((G, N_HEADS, R, KV_W), BF16),
                     jax.ShapeDtypeStruct((G, IDX_HEADS, R, IDX_DIM), BF16),
                     jax.ShapeDtypeStruct((n, LANES), F32),
                     jax.ShapeDtypeStruct((n, KV_W), F32), jax.ShapeDtypeStruct((n, KV_W), F32),
                     jax.ShapeDtypeStruct((n, KV_W), BF16), jax.ShapeDtypeStruct((n, KV_W), BF16),
                     jax.ShapeDtypeStruct((n, D_MODEL), F32), jax.ShapeDtypeStruct((n, D_MODEL), F32),
                     jax.ShapeDtypeStruct((G, R, CONV_W), F32)]
        scratch = []
    else:
        out_specs = [pl.BlockSpec((N_HEADS, tm, HEAD_DIM), lambda i: (0, i, 0)),
                     pl.BlockSpec((IDX_HEADS, tm, IDX_DIM), lambda i: (0, i, 0)),
                     row(LANES), row(KV_W), row(KV_W),
                     pl.BlockSpec((KV_W, tm), lambda i: (0, i)), pl.BlockSpec((IDX_DIM, tm), lambda i: (0, i)),
                     row(N_KV_HEADS * LANES), row(D_MODEL), row(D_MODEL),
                     pl.BlockSpec((SUBLANES, CONV_W), lambda i: (0, 0))]
        out_shape = [jax.ShapeDtypeStruct((N_HEADS, n, HEAD_DIM), BF16),
                     jax.ShapeDtypeStruct((IDX_HEADS, n, IDX_DIM), BF16),
                     jax.ShapeDtypeStruct((n, LANES), F32),
                     jax.ShapeDtypeStruct((n, KV_W), F32), jax.ShapeDtypeStruct((n, KV_W), F32),
                     jax.ShapeDtypeStruct((KV_W, n), BF16), jax.ShapeDtypeStruct((IDX_DIM, n), BF16),
                     jax.ShapeDtypeStruct((n, N_KV_HEADS * LANES), BF16),
                     jax.ShapeDtypeStruct((n, D_MODEL), F32), jax.ShapeDtypeStruct((n, D_MODEL), F32),
                     jax.ShapeDtypeStruct((SUBLANES, CONV_W), F32)]
        scratch = [pltpu.VMEM((tm + SUBLANES, CONV_W), F32)]
    return pl.pallas_call(
        functools.partial(_proj_kernel, sample=sample, tm=tm, pos_base=pos_base, cols=cols),
        grid=(nb,), in_specs=in_specs, out_specs=out_specs, out_shape=out_shape, scratch_shapes=scratch,
        compiler_params=_params("arbitrary"),
        name="proj_sample" if sample else "proj_prompt",
    )(*args)


def _lane_bcast(col):
    return jnp.broadcast_to(col, (col.shape[0], LANES))


def _lane_min(a):
    return jnp.min(a, axis=1, keepdims=True)


def _lane_max(a):
    return jnp.max(a, axis=1, keepdims=True)


def _lane_sum(a):
    return jnp.sum(a, axis=1, keepdims=True)


def _threshold_bracket(g_even, g_odd):
    lo = jnp.maximum(jnp.minimum(_lane_min(g_even), _lane_min(g_odd)), -BIG_SCORE)
    top = jnp.maximum(_lane_max(g_even), _lane_max(g_odd))
    return lo, top + jnp.maximum(jnp.abs(top) * 2.0 ** -10, 1e-30)


def _select_threshold(count_ge, max_below, lo, hi, done0, topk):
    live0 = jnp.logical_not(done0)

    def bisect(_, st):
        lo, hi, c_lo, c_hi = st
        act = live0 & (c_lo != topk)
        mid = 0.5 * lo + 0.5 * hi
        c = count_ge(mid)
        up = act & (c >= topk)
        dn = act & (c < topk)
        return jnp.where(up, mid, lo), jnp.where(dn, mid, hi), jnp.where(up, c, c_lo), jnp.where(dn, c, c_hi)

    lo, hi, c_lo, c_hi = lax.fori_loop(
        0, BISECT_STEPS, bisect, (lo, hi, jnp.full(lo.shape, INT_BIG, I32), jnp.zeros(lo.shape, I32)))

    def snap_cond(st):
        return jnp.max(st[4]) > 0

    def snap(st):
        lo, hi, c_lo, c_hi, open_ = st
        cand = max_below(hi)
        c = count_ge(cand)
        found = (open_ > 0) & (c >= topk)
        down = (open_ > 0) & (c < topk)
        return (jnp.where(found, cand, lo), jnp.where(down, cand, hi), jnp.where(found, c, c_lo),
                jnp.where(down, c, c_hi), down.astype(I32))

    lo, hi, c_lo, c_hi, _ = lax.while_loop(
        snap_cond, snap, (lo, hi, c_lo, c_hi, (live0 & (c_lo != topk)).astype(I32)))
    tie = live0 & (c_lo != topk)
    return jnp.where(done0, -BIG_SCORE, lo), tie, topk - c_hi


def _tie_cut(count_tie_below, need, r, nbits):
    def bit_step(b, x):
        cand = x + lax.shift_left(jnp.int32(1), nbits - 1 - b)
        return jnp.where(count_tie_below(cand) < r, cand, x)
    x = lax.fori_loop(0, nbits, bit_step, jnp.zeros(r.shape, I32))
    return jnp.where(need, x, INT_BIG)


def _selected(kk, kidx, thr_b, cut_b):
    return (kk > thr_b) | ((kk == thr_b) & (kidx <= cut_b))


def _attn_prompt_kernel(iq_ref, q_ref, iks_ref, ikt_ref, kt_ref, va_ref, o_ref,
                        score_scr, wb_scr, cut_scr, m_scr, qa_scr, acc_scr, kn_scr, *, tq, ck, topk, nbits):
    qi = pl.program_id(0)
    pair = CHUNKS_PER_STEP * ck
    n_pair = (qi * tq + tq + pair - 1) // pair
    nsub = ck // LANES

    w = iks_ref[:, IW_LANE:IW_LANE + IDX_HEADS] * (IDX_HEADS ** -0.5)
    for h in range(IDX_HEADS):
        wb_scr[h] = _lane_bcast(w[:, h:h + 1])
    qpos = qi * tq + lax.broadcasted_iota(I32, (tq, LANES), 0)
    lane = lax.broadcasted_iota(I32, (tq, LANES), 1)

    def for_chunks(fn, init):
        def body(c2, carry):
            for part in range(CHUNKS_PER_STEP):
                carry = fn(pl.multiple_of(c2 * pair + part * ck, ck), carry)
            return carry
        return lax.fori_loop(0, n_pair, body, init)

    def tile_of(k0):
        return lax.shift_right_logical(k0, LANES.bit_length() - 1)

    def score_chunk(k0, gmax):
        ikc = ikt_ref[:, pl.ds(k0, ck)]
        acc = jnp.zeros((tq, ck), F32)
        for h in range(IDX_HEADS):
            s = _dot(iq_ref[h], ikc)
            wbh = wb_scr[h]
            acc = acc + jnp.maximum(s, 0.0) * jnp.concatenate([wbh] * nsub, axis=1)
        gmax = list(gmax)
        for jj in range(nsub):
            kidx = k0 + jj * LANES + lane
            kj = jnp.where(kidx <= qpos, acc[:, jj * LANES:(jj + 1) * LANES], -jnp.inf)
            score_scr[tile_of(k0) + jj] = kj
            gmax[jj % 2] = jnp.maximum(gmax[jj % 2], kj)
        return tuple(gmax)
    floor = jnp.full((tq, LANES), -jnp.inf, F32)
    g_even, g_odd = for_chunks(score_chunk, (floor, floor))

    def count_ge(x_b):
        def body(k0, cnt):
            cnt = list(cnt)
            for jj in range(nsub):
                cnt[jj % 2] = cnt[jj % 2] + jnp.where(score_scr[tile_of(k0) + jj] >= x_b, 1, 0)
            return tuple(cnt)
        zero = jnp.zeros((tq, LANES), I32)
        c0, c1 = for_chunks(body, (zero, zero))
        return _lane_bcast(_lane_sum(c0 + c1))

    def max_below(x_b):
        def body(k0, m):
            for jj in range(nsub):
                kk = score_scr[tile_of(k0) + jj]
                m = jnp.maximum(m, jnp.where(kk < x_b, kk, -jnp.inf))
            return m
        return _lane_bcast(_lane_max(for_chunks(body, floor)))

    def count_tie_below(x_b):
        def body(k0, cnt):
            for jj in range(nsub):
                kk = score_scr[tile_of(k0) + jj]
                cnt = cnt + jnp.where((kk == thr_b) & (k0 + jj * LANES + lane < x_b), 1, 0)
            return cnt
        return _lane_bcast(_lane_sum(for_chunks(body, jnp.zeros((tq, LANES), I32))))

    lo0, hi0 = _threshold_bracket(g_even, g_odd)
    thr_b, tie, n_tie = _select_threshold(count_ge, max_below, _lane_bcast(lo0), _lane_bcast(hi0),
                                          qpos + 1 <= topk, topk)
    cut_scr[...] = jnp.full((tq, LANES), INT_BIG, I32)

    @pl.when(jnp.max(tie.astype(I32)) > 0)
    def _():
        cut_scr[...] = _tie_cut(count_tie_below, tie, n_tie, nbits)
    cut_b = cut_scr[...]

    def sel_tiles(k0):
        out = []
        for jj in range(nsub):
            kk = score_scr[tile_of(k0) + jj]
            sel = _selected(kk, k0 + jj * LANES + lane, thr_b, cut_b)
            out.append(jnp.concatenate([sel] * REP, axis=0))
        return out

    def group_q(g):
        return jnp.concatenate([q_ref[REP * g + r] for r in range(REP)], axis=0)

    kb = kt_ref[:, pl.ds(pl.multiple_of(qi * tq, tq), tq)].astype(F32)
    kn2 = jnp.max(jnp.sum((kb * kb).reshape(N_KV_HEADS, HEAD_DIM, tq), axis=1), axis=1, keepdims=True)

    @pl.when(qi == 0)
    def _():
        kn_scr[...] = jnp.zeros(kn_scr.shape, F32)
    kn2 = jnp.concatenate([kn2, jnp.zeros((SUBLANES - N_KV_HEADS, 1), F32)], axis=0)
    kn_scr[...] = jnp.maximum(kn_scr[...], jnp.broadcast_to(kn2, kn_scr.shape))

    lane2 = lax.broadcasted_iota(I32, (REP * tq, LANES), 1)
    ones_row = jnp.where(lax.broadcasted_iota(I32, (LANES - HEAD_DIM, ck), 0) == 0, 1.0, 0.0).astype(BF16)

    def set_shift(g, m_b):
        q2 = jnp.concatenate([group_q(g).astype(F32), jnp.zeros((REP * tq, LANES - HEAD_DIM), F32)], axis=1)
        qa_scr[g] = jnp.where(lane2 == HEAD_DIM, -m_b, q2).astype(BF16)

    def weighted_values():
        acc_scr[...] = jnp.zeros(acc_scr.shape, F32)

        def pv_chunk(k0, carry):
            sels = sel_tiles(k0)
            for g in range(N_KV_HEADS):
                k_aug = jnp.concatenate([kt_ref[HEAD_DIM * g:HEAD_DIM * (g + 1), pl.ds(k0, ck)], ones_row], axis=0)
                s = _dot(qa_scr[g], k_aug)
                ps = [jnp.where(sels[jj], jnp.exp2(s[:, jj * LANES:(jj + 1) * LANES]), 0.0).astype(BF16)
                      for jj in range(nsub)]
                acc_scr[g] += _dot(jnp.concatenate(ps, axis=1), va_ref[pl.ds(k0, ck), LANES * g:LANES * (g + 1)])
            return carry
        for_chunks(pv_chunk, 0)

    for g in range(N_KV_HEADS):
        qf = group_q(g).astype(F32)
        bound = jnp.sqrt(_lane_sum(qf * qf) * kn_scr[g:g + 1, 0:1])
        set_shift(g, _lane_bcast(bound * SHIFT_MARGIN + 1e-30))
    weighted_values()

    row_sum_ok = jnp.min(acc_scr[...][:, :, HEAD_DIM:HEAD_DIM + 1]) >= MIN_ROW_SUM

    @pl.when(jnp.logical_not(row_sum_ok))
    def _():
        m_scr[...] = jnp.full(m_scr.shape, -jnp.inf, F32)

        def max_chunk(k0, carry):
            sels = sel_tiles(k0)
            for g in range(N_KV_HEADS):
                s = _dot(group_q(g), kt_ref[HEAD_DIM * g:HEAD_DIM * (g + 1), pl.ds(k0, ck)])
                m = m_scr[g]
                for jj in range(nsub):
                    m = jnp.maximum(m, jnp.where(sels[jj], s[:, jj * LANES:(jj + 1) * LANES], -jnp.inf))
                m_scr[g] = m
            return carry
        for_chunks(max_chunk, 0)
        for g in range(N_KV_HEADS):
            set_shift(g, _lane_bcast(_lane_max(m_scr[g])))
        weighted_values()

    outs = []
    for g in range(N_KV_HEADS):
        acc = acc_scr[g]
        o = acc[:, :HEAD_DIM] / acc[:, HEAD_DIM:HEAD_DIM + 1]
        outs += [o[r * tq:(r + 1) * tq] for r in range(REP)]
    o_ref[...] = jnp.concatenate(outs, axis=1).astype(BF16)


def _attn_prompt(iq8, q8, iks, ikt, kt, va, topk):
    n = iks.shape[0]
    tq = min(128, n)
    ck = min(256, n // CHUNKS_PER_STEP)
    nbits = max(1, int(np.ceil(np.log2(n))))
    return pl.pallas_call(
        functools.partial(_attn_prompt_kernel, tq=tq, ck=ck, topk=topk, nbits=nbits),
        grid=(n // tq,),
        in_specs=[pl.BlockSpec((IDX_HEADS, tq, IDX_DIM), lambda i: (0, i, 0)),
                  pl.BlockSpec((N_HEADS, tq, HEAD_DIM), lambda i: (0, i, 0)),
                  pl.BlockSpec((tq, LANES), lambda i: (i, 0)),
                  _resident((IDX_DIM, n)), _resident((KV_W, n)), _resident((n, N_KV_HEADS * LANES))],
        out_specs=pl.BlockSpec((tq, ATTN_W), lambda i: (i, 0)),
        out_shape=jax.ShapeDtypeStruct((n, ATTN_W), BF16),
        scratch_shapes=[pltpu.VMEM((n // LANES, tq, LANES), F32),
                        pltpu.VMEM((IDX_HEADS, tq, LANES), F32),
                        pltpu.VMEM((tq, LANES), I32),
                        pltpu.VMEM((N_KV_HEADS, REP * tq, LANES), F32),
                        pltpu.VMEM((N_KV_HEADS, REP * tq, LANES), BF16),
                        pltpu.VMEM((N_KV_HEADS, REP * tq, LANES), F32),
                        pltpu.VMEM((SUBLANES, LANES), F32)],
        compiler_params=_params("arbitrary"),
        name="attn_prompt",
    )(iq8, q8, iks, ikt, kt, va)


def _attn_sample_kernel(pt_ref, iq_ref, q_ref, iks_ref, kbn_ref, vbn_ref, cik_hbm, ck_hbm, cv_hbm, o_ref,
                        ikp, kp, vp, sems, score_scr, cut_scr, *, nreq, n_pages, t_new, ppc, topk, nbits):
    b = pl.program_id(0)
    past = n_pages * PAGE_SIZE
    rows = IDX_HEADS * t_new
    qrows = nreq * t_new
    n_ck = n_pages // ppc
    ck = ppc * PAGE_SIZE
    reqs = range(nreq)

    def page_copy(src, dst, sem, slot, page):
        return pltpu.make_async_copy(src.at[page], dst.at[slot], sem)

    def start_pages(src, dst, sem):
        for r in reqs:
            def body(p, carry, r=r):
                page_copy(src, dst, sem, r * n_pages + p, pt_ref[b * nreq + r, p]).start()
                return carry
            lax.fori_loop(0, n_pages, body, 0)
    start_pages(cik_hbm, ikp, sems.at[0])
    start_pages(ck_hbm, kp, sems.at[1])
    start_pages(cv_hbm, vp, sems.at[2])

    def wait_pages(src, dst, sem):
        def body(slot, carry):
            page_copy(src, dst, sem, slot, 0).wait()
            return carry
        lax.fori_loop(0, nreq * n_pages, body, 0)

    def chunk_of(buf, r, c, width):
        return jnp.concatenate([buf[r * n_pages + c * ppc + j].reshape(width, PAGE_SIZE) for j in range(ppc)],
                               axis=1).astype(BF16)

    req_rows = lambda a, r: a[r * t_new:(r + 1) * t_new]
    iks = iks_ref[...]
    w = iks[:, IW_LANE:IW_LANE + IDX_HEADS] * (IDX_HEADS ** -0.5)
    wb = [_lane_bcast(jnp.concatenate([req_rows(w, r)[:, h:h + 1] for h in range(IDX_HEADS)], axis=0))
          for r in reqs]
    iq = [iq_ref[r].reshape(rows, IDX_DIM) for r in reqs]
    qrow = lax.broadcasted_iota(I32, (t_new, LANES), 0)
    lane = lax.broadcasted_iota(I32, (t_new, LANES), 1)
    lane_all = lax.broadcasted_iota(I32, (qrows, LANES), 1)

    def head_sum(t):
        acc = t[0:t_new]
        for h in range(1, IDX_HEADS):
            acc = acc + t[h * t_new:(h + 1) * t_new]
        return acc

    wait_pages(cik_hbm, ikp, sems.at[0])

    def score_chunk(c, gmax):
        gmax = list(gmax)
        for r in reqs:
            s = _dot(iq[r], chunk_of(ikp, r, c, IDX_DIM))
            for jj in range(ppc):
                kj = head_sum(jnp.maximum(s[:, jj * LANES:(jj + 1) * LANES], 0.0) * wb[r])
                score_scr[r * t_new:(r + 1) * t_new, pl.ds(pl.multiple_of(c * ck, ck) + jj * LANES, LANES)] = kj
                gmax[2 * r + jj % 2] = jnp.maximum(gmax[2 * r + jj % 2], kj)
        return tuple(gmax)
    floor = jnp.full((t_new, LANES), -jnp.inf, F32)
    gmax = lax.fori_loop(0, n_ck, score_chunk, (floor,) * (2 * nreq))

    pad_rows = lambda a: jnp.concatenate([a, jnp.zeros((LANES - t_new, a.shape[1]), a.dtype)], axis=0)
    new_ok = (lane <= qrow) & (lane < t_new)
    k_new = []
    for r in reqs:
        ik_r = req_rows(iks, r)[:, :IDX_DIM].astype(BF16)
        s_new = jnp.maximum(_dot_t(iq[r], pad_rows(ik_r)), 0.0) * wb[r]
        k_new.append(jnp.where(new_ok, head_sum(s_new), -jnp.inf))
    k_new_all = jnp.concatenate(k_new, axis=0)
    score_scr[:, past:past + LANES] = k_new_all

    n_tiles = past // LANES + 1
    tile_idx = lambda jt: jt * LANES + lane_all

    def count_over(pred):
        def run(cand):
            cand_b = _lane_bcast(cand)
            cnt = [jnp.zeros((qrows, LANES), I32)] * 2
            for jt in range(n_tiles):
                kk = score_scr[:, jt * LANES:(jt + 1) * LANES]
                cnt[jt % 2] = cnt[jt % 2] + jnp.where(pred(kk, tile_idx(jt), cand_b), 1, 0)
            return _lane_sum(cnt[0] + cnt[1])
        return run

    def max_below(x):
        x_b = _lane_bcast(x)
        m = jnp.full((qrows, LANES), -jnp.inf, F32)
        for jt in range(n_tiles):
            kk = score_scr[:, jt * LANES:(jt + 1) * LANES]
            m = jnp.maximum(m, jnp.where(kk < x_b, kk, -jnp.inf))
        return _lane_max(m)

    g_even = jnp.concatenate([jnp.maximum(gmax[2 * r], k_new[r]) for r in reqs], axis=0)
    g_odd = jnp.concatenate([gmax[2 * r + 1] for r in reqs], axis=0)
    lo0, hi0 = _threshold_bracket(g_even, g_odd)
    done0 = jnp.zeros((qrows, 1), jnp.bool_)
    thr, tie, n_tie = _select_threshold(count_over(lambda kk, kidx, cb: kk >= cb), max_below, lo0, hi0, done0, topk)
    thr_b = _lane_bcast(thr)
    cut_scr[...] = jnp.full((qrows, LANES), INT_BIG, I32)

    @pl.when(jnp.max(tie.astype(I32)) > 0)
    def _():
        cut = _tie_cut(count_over(lambda kk, kidx, cb: (kk == thr_b) & (kidx < cb)), tie, n_tie, nbits)
        cut_scr[...] = _lane_bcast(cut)
    cut_b = cut_scr[...]

    def sel_rows(r, kk, kidx):
        sel = _selected(kk, kidx, req_rows(thr_b, r), req_rows(cut_b, r))
        return jnp.concatenate([sel] * N_HEADS, axis=0)

    def sel_chunk(r, c, jj):
        kk = score_scr[r * t_new:(r + 1) * t_new, pl.ds(pl.multiple_of(c * ck, ck) + jj * LANES, LANES)]
        return sel_rows(r, kk, c * ck + jj * LANES + lane)

    wait_pages(ck_hbm, kp, sems.at[1])
    wait_pages(cv_hbm, vp, sems.at[2])
    q = [q_ref[r].reshape(rows, KV_W) for r in reqs]
    s_new = [_dot_t(q[r], pad_rows(req_rows(kbn_ref[...], r))) for r in reqs]
    sel_new = [sel_rows(r, k_new[r], past + lane) for r in reqs]

    def max_chunk(c, ms):
        ms = list(ms)
        for r in reqs:
            s = _dot(q[r], chunk_of(kp, r, c, KV_W))
            for jj in range(ppc):
                ms[r] = jnp.maximum(ms[r], jnp.where(sel_chunk(r, c, jj), s[:, jj * LANES:(jj + 1) * LANES], -jnp.inf))
        return tuple(ms)
    ms = lax.fori_loop(0, n_ck, max_chunk, tuple(jnp.where(sel_new[r], s_new[r], -jnp.inf) for r in reqs))
    m_b = [_lane_bcast(_lane_max(ms[r])) for r in reqs]

    p_new = [jnp.where(sel_new[r], jnp.exp2(s_new[r] - m_b[r]), 0.0) for r in reqs]
    acc0 = [_dot(p_new[r].astype(BF16), pad_rows(req_rows(vbn_ref[...], r))) for r in reqs]

    def pv_chunk(c, carry):
        ls, accs = list(carry[0]), list(carry[1])
        for r in reqs:
            s = _dot(q[r], chunk_of(kp, r, c, KV_W))
            ps = []
            for jj in range(ppc):
                p = jnp.where(sel_chunk(r, c, jj), jnp.exp2(s[:, jj * LANES:(jj + 1) * LANES] - m_b[r]), 0.0)
                ls[r] = ls[r] + p
                ps.append(p.astype(BF16))
            accs[r] = accs[r] + _dot_t(jnp.concatenate(ps, axis=1), chunk_of(vp, r, c, KV_W))
        return tuple(ls), tuple(accs)
    ls, accs = lax.fori_loop(0, n_ck, pv_chunk, (tuple(p_new), tuple(acc0)))
    for r in reqs:
        o = accs[r] / _lane_sum(ls[r])
        o_ref[r] = o.reshape(N_HEADS, t_new, KV_W).astype(BF16)


def _attn_sample(page_table, iq_s, q_s, iks_s, kb_s, vb_s, cache_ikt, cache_kt, cache_vt, topk):
    db, n_pages = page_table.shape
    t_new = iq_s.shape[2]
    past = n_pages * PAGE_SIZE
    assert past >= 2 * LANES and past + 1 > topk, "the threshold bracket needs two full key tiles"
    ppc = 64 if n_pages % 64 == 0 else 2
    nreq = 2 if db % 2 == 0 else 1
    nbits = max(1, int(np.ceil(np.log2(past + LANES))))
    grid_spec = pltpu.PrefetchScalarGridSpec(
        num_scalar_prefetch=1,
        grid=(db // nreq,),
        in_specs=[pl.BlockSpec((nreq, IDX_HEADS, t_new, IDX_DIM), lambda b, pt: (b, 0, 0, 0)),
                  pl.BlockSpec((nreq, N_HEADS, t_new, KV_W), lambda b, pt: (b, 0, 0, 0)),
                  pl.BlockSpec((nreq * t_new, LANES), lambda b, pt: (b, 0)),
                  pl.BlockSpec((nreq * t_new, KV_W), lambda b, pt: (b, 0)),
                  pl.BlockSpec((nreq * t_new, KV_W), lambda b, pt: (b, 0)),
                  pl.BlockSpec(memory_space=pl.ANY), pl.BlockSpec(memory_space=pl.ANY),
                  pl.BlockSpec(memory_space=pl.ANY)],
        out_specs=pl.BlockSpec((nreq, N_HEADS, t_new, KV_W), lambda b, pt: (b, 0, 0, 0)),
        scratch_shapes=[pltpu.VMEM((nreq * n_pages, IDX_DIM, PAGE_SIZE), F32),
                        pltpu.VMEM((nreq * n_pages, N_KV_HEADS, HEAD_DIM, PAGE_SIZE), F32),
                        pltpu.VMEM((nreq * n_pages, N_KV_HEADS, HEAD_DIM, PAGE_SIZE), F32),
                        pltpu.SemaphoreType.DMA((3,)),
                        pltpu.VMEM((nreq * t_new, past + LANES), F32),
                        pltpu.VMEM((nreq * t_new, LANES), I32)])
    return pl.pallas_call(
        functools.partial(_attn_sample_kernel, nreq=nreq, n_pages=n_pages, t_new=t_new, ppc=ppc, topk=topk,
                          nbits=nbits),
        grid_spec=grid_spec,
        out_shape=jax.ShapeDtypeStruct((db, N_HEADS, t_new, KV_W), BF16),
        compiler_params=_params("arbitrary"),
        name="attn_sample",
    )(page_table, iq_s, q_s, iks_s, kb_s, vb_s, cache_ikt, cache_kt, cache_vt)


def _mix_kernel(attn_ref, sga_ref, yc_ref, x_ref, g1_ref, sh2_ref, sc2_ref, n2_ref, woa_ref, wout_ref,
                wr_hi_ref, wr_lo_ref, br_ref, x1_ref, h2_ref, comb_ref, *, sample, tm):
    if sample:
        a = attn_ref[...]
        y_attn = jnp.zeros((tm, D_MODEL), F32)
        for h in range(N_HEADS):
            y_attn = y_attn + _dot(a[:, h].reshape(tm, KV_W), woa_ref[h])
    else:
        y_attn = _dot(attn_ref[...], woa_ref[...])
    mixed = _dot((sga_ref[...] * y_attn + yc_ref[...]).astype(BF16), wout_ref[...])
    x3 = x_ref[...]
    g, r, _ = x3.shape
    x1 = x3 + g1_ref[...] * mixed.reshape(g, r, D_MODEL)
    x1_ref[...] = x1
    h2 = _modnorm(x1, n2_ref[...], sc2_ref[...], sh2_ref[...])
    h2_ref[...] = h2.astype(BF16)

    lg = _dot3(h2, wr_hi_ref[...], wr_lo_ref[...]) + br_ref[...]
    lane = lax.broadcasted_iota(I32, (tm, LANES), 1)
    is_g = lane < N_GROUPS
    is_e = (lane >= ROUTER_E0) & (lane < ROUTER_E0 + N_EXPERTS)
    first = lambda msk: _lane_min(jnp.where(msk, lane, LANES))
    gl = jnp.where(is_g, lg, -jnp.inf)
    gmax = _lane_max(gl)
    g_sel = first(gl == gmax)
    g_w = 1.0 / _lane_sum(jnp.where(is_g, jnp.exp(lg - gmax), 0.0))
    in_grp = is_e & (((lane - ROUTER_E0) >> 2) == g_sel)
    me = jnp.where(in_grp, lg, -jnp.inf)
    v1 = _lane_max(me)
    i1 = first(in_grp & (me == v1))
    rest = in_grp & (lane != i1)
    me2 = jnp.where(rest, lg, -jnp.inf)
    v2 = _lane_max(me2)
    i2 = first(rest & (me2 == v2))
    e2 = jnp.exp(v2 - v1)
    den = 1.0 + e2
    comb_ref[...] = jnp.where(lane == i1, g_w / den, 0.0) + jnp.where(lane == i2, g_w * e2 / den, 0.0)


def _mix(attn, sga, yc, x3, mod3, mod_row0, norm2_g, w_oa, w_out, wr_hi, wr_lo, b_r, *, sample):
    G, R, _ = x3.shape
    n = G * R
    if sample:
        bb = min(G, 32)
        tm = bb * R
        nb = G // bb
        xspec = pl.BlockSpec((bb, R, D_MODEL), lambda i: (i, 0, 0))
        mspec = lambda c: pl.BlockSpec((bb, 1, D_MODEL), lambda i: (i, 0, c))
        aspec = pl.BlockSpec((bb, N_HEADS, R, KV_W), lambda i: (i, 0, 0, 0))
        woa_spec = _resident((N_HEADS, KV_W, D_MODEL))
    else:
        tm = min(R, 512)
        nb = R // tm
        xspec = pl.BlockSpec((1, tm, D_MODEL), lambda i: (0, i, 0))
        mspec = lambda c: pl.BlockSpec((1, 1, D_MODEL), lambda i: (mod_row0, 0, c))
        aspec = pl.BlockSpec((tm, ATTN_W), lambda i: (i, 0))
        woa_spec = _resident((ATTN_W, D_MODEL))
    row = lambda w: pl.BlockSpec((tm, w), lambda i: (i, 0))
    return pl.pallas_call(
        functools.partial(_mix_kernel, sample=sample, tm=tm),
        grid=(nb,),
        in_specs=[aspec, row(D_MODEL), row(D_MODEL), xspec, mspec(2), mspec(3), mspec(4), _resident((1, D_MODEL)),
                  woa_spec, _resident((D_MODEL, D_MODEL)), _resident((D_MODEL, LANES)), _resident((D_MODEL, LANES)),
                  _resident((1, LANES))],
        out_specs=[xspec, row(D_MODEL), row(LANES)],
        out_shape=[jax.ShapeDtypeStruct((G, R, D_MODEL), F32), jax.ShapeDtypeStruct((n, D_MODEL), BF16),
                   jax.ShapeDtypeStruct((n, LANES), F32)],
        compiler_params=_params("arbitrary"),
        name="mix_sample" if sample else "mix_prompt",
    )(attn, sga, yc, x3, mod3, mod3, mod3, norm2_g.reshape(1, D_MODEL), w_oa, w_out, wr_hi, wr_lo, b_r)


def _moe_kernel(h2_ref, comb_ref, x1_ref, g2_ref, fg_ref, weg_ref, weu_ref, wed_ref, y_ref, acc_scr, *, tm, epb):
    j = pl.program_id(1)

    @pl.when(j == 0)
    def _():
        acc_scr[...] = jnp.zeros(acc_scr.shape, F32)

    h2 = h2_ref[...]
    a = _dot(h2, weg_ref[...])
    u = _dot(h2, weu_ref[...])
    hid = a * _sigmoid(a) * u
    comb = comb_ref[...]
    lane = lax.broadcasted_iota(I32, (tm, LANES), 1)
    parts = []
    for e in range(epb):
        ce = _lane_sum(jnp.where(lane == ROUTER_E0 + j * epb + e, comb, 0.0))
        parts.append((hid[:, e * EXPERT_FF:(e + 1) * EXPERT_FF] * ce).astype(BF16))
    acc_scr[...] += _dot(jnp.concatenate(parts, axis=1), wed_ref[...])

    @pl.when(j == pl.num_programs(1) - 1)
    def _():
        x1 = x1_ref[...]
        g, r, _ = x1.shape
        x2 = x1 + g2_ref[...] * acc_scr[...].reshape(g, r, D_MODEL)
        ms = jnp.mean(x2 * x2, axis=-1, keepdims=True)
        y_ref[...] = x2 * lax.rsqrt(ms + EPS) * fg_ref[...][None]


def _moe(h2, comb, x1, mod3, mod_row0, final_g, w_eg, w_eu, w_ed, *, sample):
    G, R, _ = x1.shape
    epb = EXP_PER_GROUP
    nj = N_EXPERTS // epb
    wblk = epb * EXPERT_FF
    if sample:
        bb = min(G, 64)
        tm = bb * R
        nb = G // bb
        xspec = pl.BlockSpec((bb, R, D_MODEL), lambda i, j: (i, 0, 0))
        g2spec = pl.BlockSpec((bb, 1, D_MODEL), lambda i, j: (i, 0, 5))
    else:
        tm = min(R, 1024)
        nb = R // tm
        xspec = pl.BlockSpec((1, tm, D_MODEL), lambda i, j: (0, i, 0))
        g2spec = pl.BlockSpec((1, 1, D_MODEL), lambda i, j: (mod_row0, 0, 5))
    return pl.pallas_call(
        functools.partial(_moe_kernel, tm=tm, epb=epb),
        grid=(nb, nj),
        in_specs=[pl.BlockSpec((tm, D_MODEL), lambda i, j: (i, 0)),
                  pl.BlockSpec((tm, LANES), lambda i, j: (i, 0)),
                  xspec, g2spec,
                  pl.BlockSpec((1, D_MODEL), lambda i, j: (0, 0)),
                  pl.BlockSpec((D_MODEL, wblk), lambda i, j: (0, j)),
                  pl.BlockSpec((D_MODEL, wblk), lambda i, j: (0, j)),
                  pl.BlockSpec((wblk, D_MODEL), lambda i, j: (j, 0))],
        out_specs=xspec,
        out_shape=jax.ShapeDtypeStruct((G, R, D_MODEL), F32),
        scratch_shapes=[pltpu.VMEM((tm, D_MODEL), F32)],
        compiler_params=_params("arbitrary", "arbitrary"),
        name="moe_sample" if sample else "moe_prompt",
    )(h2, comb, x1, mod3, final_g.reshape(1, D_MODEL), w_eg, w_eu, w_ed)


def kernel(x_prompt, x_sample, cache_k, cache_v, cache_idx_k, state_conv, page_table, c_prompt, c_sample,
           w_ada, b_ada, norm1_g, norm2_g, w_in_mix, conv_w, w_o_attn, w_o_conv, w_out,
           w_router_group, b_router_group, w_router_expert, b_router_expert,
           w_exp_gate, w_exp_up, w_exp_down, final_norm_g):
    depth = w_ada.shape[0]
    assert depth == 1, "single-layer step"
    B, S, _ = x_prompt.shape
    assert B == 1, "one prompt sequence"
    DB, T, _ = x_sample.shape
    assert T == SUBLANES, "sample tokens per request must fill one sublane tile"
    n_pages = page_table.shape[1]
    past = n_pages * PAGE_SIZE

    n_rows = DB + B
    pad = (-n_rows) % SUBLANES
    c_all = jnp.concatenate([c_sample, c_prompt, jnp.zeros((pad, D_MODEL), F32)], axis=0)
    mod = _ada(c_all, w_ada[0], b_ada[0])
    mod3 = mod.reshape(n_rows + pad, 1, 6 * D_MODEL)

    half = HEAD_DIM // 8
    inv_freq = ROPE_THETA ** (-jnp.arange(half, dtype=F32) / half)
    lane = np.arange(LANES)
    invf = jnp.where((lane % HEAD_DIM) < 2 * half, inv_freq[lane % half], 0.0).reshape(1, LANES).astype(F32)
    w_in = w_in_mix[0]
    w_oc = w_o_conv[0].astype(BF16)
    w_oa = w_o_attn[0]
    w_oa_p = w_oa.astype(BF16)
    w_oa_s = jnp.stack([
        jnp.concatenate([jnp.zeros((HEAD_DIM * (h // REP), D_MODEL), F32), w_oa[HEAD_DIM * h:HEAD_DIM * (h + 1)],
                         jnp.zeros((KV_W - HEAD_DIM * (h // REP + 1), D_MODEL), F32)], axis=0)
        for h in range(N_HEADS)]).astype(BF16)
    w_o = w_out[0].astype(BF16)
    zr = jnp.zeros((D_MODEL, LANES - ROUTER_E0 - N_EXPERTS), F32)
    w_r = jnp.concatenate([w_router_group[0], w_router_expert[0], zr], axis=1)
    wr_hi = w_r.astype(BF16)
    wr_lo = (w_r - wr_hi.astype(F32)).astype(BF16)
    b_r = jnp.concatenate([b_router_group[0], b_router_expert[0],
                           jnp.zeros((LANES - ROUTER_E0 - N_EXPERTS,), F32)]).reshape(1, LANES)
    cat_e = lambda w: jnp.transpose(w, (1, 0, 2)).reshape(D_MODEL, N_EXPERTS * EXPERT_FF).astype(BF16)
    w_eg = cat_e(w_exp_gate[0])
    w_eu = cat_e(w_exp_up[0])
    w_ed = w_exp_down[0].reshape(N_EXPERTS * EXPERT_FF, D_MODEL).astype(BF16)

    xp3 = x_prompt
    (q8, iq8, iks_p, k_p, v_p, kt, ikt, va_p, yc_p, sga_p, cst_p) = _proj(
        xp3, mod3, DB, norm1_g[0], _proj_weight(w_in, False), invf, conv_w[0], w_oc, None, sample=False, pos_base=0)
    attn_p = _attn_prompt(iq8, q8, iks_p, ikt, kt, va_p, min(TOPK_MAX, S // 4))
    x1_p, h2_p, comb_p = _mix(attn_p, sga_p, yc_p, xp3, mod3, DB, norm2_g[0], w_oa_p, w_o, wr_hi, wr_lo, b_r,
                              sample=False)
    y_p = _moe(h2_p, comb_p, x1_p, mod3, DB, final_norm_g, w_eg, w_eu, w_ed, sample=False)

    cache_ikt = jnp.transpose(cache_idx_k[0], (0, 2, 1))
    cache_kt = jnp.transpose(cache_k[0], (0, 2, 3, 1))
    cache_vt = jnp.transpose(cache_v[0], (0, 2, 3, 1))
    (q_s, iq_s, iks_s, k_s, v_s, kb_s, vb_s, yc_s, sga_s, cst_s) = _proj(
        x_sample, mod3, 0, norm1_g[0], _proj_weight(w_in, True), invf, conv_w[0], w_oc, state_conv[0],
        sample=True, pos_base=past)
    attn_s = _attn_sample(page_table, iq_s, q_s, iks_s, kb_s, vb_s, cache_ikt, cache_kt, cache_vt,
                          min(TOPK_MAX, (past + T) // 4))
    x1_s, h2_s, comb_s = _mix(attn_s, sga_s, yc_s, x_sample, mod3, 0, norm2_g[0], w_oa_s, w_o, wr_hi, wr_lo, b_r,
                              sample=True)
    y_s = _moe(h2_s, comb_s, x1_s, mod3, 0, final_norm_g, w_eg, w_eu, w_ed, sample=True)

    return (y_p, y_s,
            k_p.reshape(1, B, S, N_KV_HEADS, HEAD_DIM), v_p.reshape(1, B, S, N_KV_HEADS, HEAD_DIM),
            iks_p[:, :IDX_DIM].reshape(1, B, S, IDX_DIM),
            cst_p[SUBLANES - (CONV_K - 1):].reshape(1, B, CONV_K - 1, CONV_W),
            k_s.reshape(1, DB, T, N_KV_HEADS, HEAD_DIM), v_s.reshape(1, DB, T, N_KV_HEADS, HEAD_DIM),
            iks_s[:, :IDX_DIM].reshape(1, DB, T, IDX_DIM),
            cst_s[:, T - (CONV_K - 1):].reshape(1, DB, CONV_K - 1, CONV_W))
```
